```python
import math
import jax, jax.numpy as jnp
from jax import lax
import numpy as np

D_MODEL = 1024
BATCH = 2
SEQ = 8192
DEPTH = 2

N_Q_HEADS = 16
N_KV_HEADS = 2
HEAD_DIM = 64
Q_GROUP = N_Q_HEADS // N_KV_HEADS
ATTN_WIDTH = N_Q_HEADS * HEAD_DIM
KV_WIDTH = N_KV_HEADS * HEAD_DIM
WINDOW = 128
BLOCK = 128
NEG_INF = -1e30
NUM_BUCKETS = 32
MAX_DISTANCE = 128
POOL_WINDOWS = (2, 4, 8, 16)
N_POOL_GROUPS = 4
POOL_WIDTH = D_MODEL
POOL_GROUP_WIDTH = POOL_WIDTH // N_POOL_GROUPS
DEEPNORM_ALPHA = (2 * DEPTH) ** 0.25
DEEPNORM_BETA = (8 * DEPTH) ** -0.25
LN_EPS = 1e-5
SPLITS = (ATTN_WIDTH, KV_WIDTH, KV_WIDTH, ATTN_WIDTH, POOL_WIDTH, POOL_WIDTH, D_MODEL, D_MODEL)
IN_WIDTH = sum(SPLITS)

kernel_name = "hybrid_swa_sink_pool_deepnorm_adaln"


def layer_norm(x, gain, bias):
    xf = x.astype(jnp.float32)
    mu = jnp.mean(xf, axis=-1, keepdims=True)
    var = jnp.mean(jnp.square(xf - mu), axis=-1, keepdims=True)
    y = (xf - mu) * lax.rsqrt(var + LN_EPS)
    return (y * gain.astype(jnp.float32) + bias.astype(jnp.float32)).astype(x.dtype)


def t5_causal_bucket(dist):
    max_exact = NUM_BUCKETS // 2
    is_small = dist < max_exact
    d = jnp.maximum(dist, 1).astype(jnp.float32)
    large = max_exact + (jnp.log(d / max_exact) / math.log(MAX_DISTANCE / max_exact)
                         * (NUM_BUCKETS - max_exact)).astype(jnp.int32)
    large = jnp.minimum(large, NUM_BUCKETS - 1)
    return jnp.where(is_small, dist, large)


def band_structure(n_blocks, rel_bias):
    q_loc = jnp.arange(BLOCK)[:, None]
    k_loc = jnp.arange(2 * BLOCK)[None, :]
    dist = q_loc + BLOCK - k_loc
    in_window = (dist >= 0) & (dist < WINDOW)
    key_exists = (jnp.arange(n_blocks)[:, None, None] > 0) | (k_loc[None] >= BLOCK)
    mask = in_window[None] & key_exists
    bucket = t5_causal_bucket(jnp.maximum(dist, 0))
    bias = rel_bias.astype(jnp.float32)[bucket]
    bias = jnp.transpose(bias, (2, 0, 1)).reshape(N_KV_HEADS, Q_GROUP, BLOCK, 2 * BLOCK)
    return mask, bias


def sliding_window_sink_attention(q, k, v, sinks, bias, mask):
    B, S, _ = q.shape
    nb = S // BLOCK
    qb = q.reshape(B, nb, BLOCK, N_KV_HEADS, Q_GROUP, HEAD_DIM)

    def band(t):
        t = t.reshape(B, S, N_KV_HEADS, HEAD_DIM)
        t = jnp.pad(t, ((0, 0), (BLOCK, 0), (0, 0), (0, 0)))
        t = t.reshape(B, nb + 1, BLOCK, N_KV_HEADS, HEAD_DIM)
        return jnp.concatenate([t[:, :-1], t[:, 1:]], axis=2)

    kb, vb = band(k), band(v)
    logits = jnp.einsum('bnqhgd,bnkhd->bnhgqk', qb, kb).astype(jnp.float32) * (HEAD_DIM ** -0.5) + bias
    logits = jnp.where(mask[None, :, None, None], logits, NEG_INF)
    sink = jnp.broadcast_to(sinks.astype(jnp.float32).reshape(N_KV_HEADS, Q_GROUP, 1, 1),
                            logits.shape[:-1] + (1,))
    probs = jax.nn.softmax(jnp.concatenate([logits, sink], axis=-1), axis=-1)[..., :-1]
    out = jnp.einsum('bnhgqk,bnkhd->bnqhgd', probs.astype(v.dtype), vb)
    return out.reshape(B, S, ATTN_WIDTH)


def multiscale_causal_pool(u):
    B, S, _ = u.shape
    ug = u.reshape(B, S, N_POOL_GROUPS, POOL_GROUP_WIDTH).astype(jnp.float32)
    cs = jnp.concatenate([jnp.zeros((B, 1, N_POOL_GROUPS, POOL_GROUP_WIDTH), jnp.float32),
                          jnp.cumsum(ug, axis=1)], axis=1)
    outs = []
    for gi, w in enumerate(POOL_WINDOWS):
        csg = cs[:, :, gi]
        lo = jnp.pad(csg, ((0, 0), (w - 1, 0), (0, 0)))[:, :S]
        count = jnp.minimum(jnp.arange(1, S + 1), w).astype(jnp.float32)[None, :, None]
        outs.append((csg[:, 1:] - lo) / count - ug[:, :, gi])
    return jnp.stack(outs, axis=2).astype(u.dtype)


def setup_inputs(seed: int = 0) -> dict:
    key = jax.random.key(seed)
    ks = jax.random.split(key, 16)
    f32 = jnp.float32
    n = lambda k, shape: jax.random.normal(k, shape, f32)
    return {
        "x": n(ks[0], (BATCH, SEQ, D_MODEL)),
        "c": n(ks[1], (BATCH, D_MODEL)),
        "rel_bias": 0.5 * n(ks[2], (NUM_BUCKETS, N_Q_HEADS)),
        "w_ada": 0.1 * D_MODEL ** -0.5 * n(ks[3], (DEPTH, D_MODEL, 3 * D_MODEL)),
        "b_ada": 0.01 * n(ks[4], (DEPTH, 3 * D_MODEL)),
        "w_in": D_MODEL ** -0.5 * n(ks[5], (DEPTH, D_MODEL, IN_WIDTH)),
        "sinks": n(ks[6], (DEPTH, N_Q_HEADS)),
        "w_pool_mix": POOL_GROUP_WIDTH ** -0.5 * n(ks[7], (DEPTH, N_POOL_GROUPS, POOL_GROUP_WIDTH, POOL_GROUP_WIDTH)),
        "pool_scale": 1.0 + 0.1 * n(ks[8], (DEPTH, POOL_WIDTH)),
        "w_attn_proj": DEEPNORM_BETA * ATTN_WIDTH ** -0.5 * n(ks[9], (DEPTH, ATTN_WIDTH, D_MODEL)),
        "w_pool_proj": DEEPNORM_BETA * POOL_WIDTH ** -0.5 * n(ks[10], (DEPTH, POOL_WIDTH, D_MODEL)),
        "w_out": DEEPNORM_BETA * D_MODEL ** -0.5 * n(ks[11], (DEPTH, D_MODEL, D_MODEL)),
        "ln_gain": 1.0 + 0.02 * n(ks[12], (DEPTH, D_MODEL)),
        "ln_bias": 0.02 * n(ks[13], (DEPTH, D_MODEL)),
    }


def reference(x, c, rel_bias, w_ada, b_ada, w_in, sinks, w_pool_mix, pool_scale,
              w_attn_proj, w_pool_proj, w_out, ln_gain, ln_bias):
    B, S, _ = x.shape
    mask, bias = band_structure(S // BLOCK, rel_bias)
    offsets = [int(o) for o in np.cumsum(SPLITS)[:-1]]
    for l in range(DEPTH):
        mod = jax.nn.silu(c) @ w_ada[l] + b_ada[l]
        shift, scale, gate = jnp.split(mod, 3, axis=-1)
        u = x * (1.0 + scale[:, None, :]) + shift[:, None, :]
        h = u @ w_in[l]
        q, k, v, a_gate, p_in, p_gate, m_a, m_p = jnp.split(h, offsets, axis=-1)
        a = sliding_window_sink_attention(q, k, v, sinks[l], bias, mask) * jax.nn.silu(a_gate)
        a = a @ w_attn_proj[l]
        p = jnp.einsum('bsgc,gcd->bsgd', multiscale_causal_pool(p_in), w_pool_mix[l]).reshape(B, S, POOL_WIDTH)
        p = (p * pool_scale[l]) * jax.nn.silu(p_gate)
        p = p @ w_pool_proj[l]
        y = jax.nn.sigmoid(m_a) * a + jax.nn.sigmoid(m_p) * p
        y = (y @ w_out[l]) * (1.0 + gate[:, None, :])
        x = layer_norm(DEEPNORM_ALPHA * x + y, ln_gain[l], ln_bias[l])
    return x
```

```python
import functools
import math

import numpy as np
import jax
import jax.numpy as jnp
from jax import lax
from jax.experimental import pallas as pl
from jax.experimental.pallas import tpu as pltpu

D_MODEL = 1024
DEPTH = 2
N_Q_HEADS = 16
N_KV_HEADS = 2
HEAD_DIM = 64
Q_GROUP = N_Q_HEADS // N_KV_HEADS
WINDOW = 128
BLOCK = 128
NEG_INF = -1e30
NUM_BUCKETS = 32
MAX_DISTANCE = 128
POOL_WINDOWS = (2, 4, 8, 16)
POOL_GROUP_WIDTH = D_MODEL // len(POOL_WINDOWS)
POOL_HALO = 16
DEEPNORM_ALPHA = (2 * DEPTH) ** 0.25
LN_EPS = 1e-5

OFF_Q = 0
OFF_KV = 1024
OFF_AGATE = 1280
OFF_PIN = 2304
OFF_PGATE = 3328
OFF_MERGE = 4352
IN_WIDTH = 6400

LANES = 128
HEADS_PER_LANE_TILE = LANES // HEAD_DIM
LANE_TILES_PER_GROUP = Q_GROUP // HEADS_PER_LANE_TILE
VMEM_LIMIT_BYTES = 56 * 1024 * 1024

F32 = jnp.float32
BF16 = jnp.bfloat16


def _bucket_map():
    q = np.arange(BLOCK)[:, None]
    k = np.arange(2 * BLOCK)[None, :]
    dist = np.maximum(q + BLOCK - k, 0)
    max_exact = NUM_BUCKETS // 2
    d = np.maximum(dist, 1).astype(np.float64)
    large = max_exact + (np.log(d / max_exact) / math.log(MAX_DISTANCE / max_exact)
                         * (NUM_BUCKETS - max_exact)).astype(np.int32)
    large = np.minimum(large, NUM_BUCKETS - 1)
    return np.where(dist < max_exact, dist, large).astype(np.int32)


def _sigmoid(x):
    return 1.0 / (1.0 + jnp.exp(-x))


def _silu(x):
    return x * _sigmoid(x)


def _adaln_kernel(c_ref, w_ref, b_ref, o_ref):
    sc = _silu(c_ref[...])
    o_ref[0] = jnp.dot(sc, w_ref[0], preferred_element_type=F32,
                       precision=lax.Precision.HIGHEST) + b_ref[0]


def _adaln(c, w_ada, b_ada):
    batch = c.shape[0]
    tn = 1024
    n_out = w_ada.shape[-1]
    return pl.pallas_call(
        _adaln_kernel,
        grid=(DEPTH, n_out // tn),
        in_specs=[
            pl.BlockSpec((batch, D_MODEL), lambda l, n: (0, 0)),
            pl.BlockSpec((1, D_MODEL, tn), lambda l, n: (l, 0, n)),
            pl.BlockSpec((1, 1, tn), lambda l, n: (l, 0, n)),
        ],
        out_specs=pl.BlockSpec((1, batch, tn), lambda l, n: (l, 0, n)),
        out_shape=jax.ShapeDtypeStruct((DEPTH, batch, n_out), F32),
        name="adaln_mod",
    )(c, w_ada, b_ada.reshape(DEPTH, 1, n_out))


def _bias_kernel(rel_ref, bucket_ref, o_ref):
    h = pl.program_id(0)
    bucket = bucket_ref[...]
    acc = jnp.zeros(bucket.shape, F32)
    for b in range(NUM_BUCKETS):
        acc = jnp.where(bucket == b, rel_ref[b, h], acc)
    o_ref[0] = acc


def _bias_table(rel_bias):
    bucket = jnp.asarray(_bucket_map())
    return pl.pallas_call(
        _bias_kernel,
        grid=(N_Q_HEADS,),
        in_specs=[
            pl.BlockSpec(memory_space=pltpu.SMEM),
            pl.BlockSpec((BLOCK, 2 * BLOCK), lambda h: (0, 0)),
        ],
        out_specs=pl.BlockSpec((1, BLOCK, 2 * BLOCK), lambda h: (h, 0, 0)),
        out_shape=jax.ShapeDtypeStruct((N_Q_HEADS, BLOCK, 2 * BLOCK), F32),
        name="t5_bias_table",
    )(rel_bias, bucket)


def _layer_kernel(x_ref, mod_ref, sinks_ref, bias_ref, w_in_ref, w_ap_ref, w_pp_ref, w_out_ref,
                  w_mix_ref, pscale_ref, lng_ref, lnb_ref, o_ref,
                  u_ref, q_ref, kp_ref, vp_ref, p_ref, sg_ref, ain_ref, pin_ref, *, tm):
    i = pl.program_id(1)
    nblk = tm // BLOCK

    @pl.when(i == 0)
    def _():
        kp_ref[:, 0:BLOCK, :] = jnp.zeros((4, BLOCK, LANES), BF16)
        vp_ref[:, 0:BLOCK, :] = jnp.zeros((4, BLOCK, LANES), BF16)
        pin_ref[0:POOL_HALO, :] = jnp.zeros((POOL_HALO, D_MODEL), F32)

    x = x_ref[0]
    shift = mod_ref[0, 0:1, :]
    scale = mod_ref[0, 1:2, :]
    gate = mod_ref[0, 2:3, :]
    u_ref[...] = (x * (1.0 + scale) + shift).astype(BF16)

    def proj(lo, hi):
        return jnp.dot(u_ref[...], w_in_ref[:, lo:hi], preferred_element_type=F32)

    qkv = proj(OFF_Q, OFF_AGATE)
    q_ref[...] = (qkv[:, 0:OFF_KV] * (HEAD_DIM ** -0.5)).astype(BF16)
    lane = lax.broadcasted_iota(jnp.int32, (tm, LANES), 1)
    first_half = lane < HEAD_DIM
    for src, dst in ((qkv[:, OFF_KV:OFF_KV + LANES], kp_ref),
                     (qkv[:, OFF_KV + LANES:OFF_KV + 2 * LANES], vp_ref)):
        g0_lo = jnp.where(first_half, src, 0.0)
        g1_hi = jnp.where(first_half, 0.0, src)
        dst[0, BLOCK:, :] = g0_lo.astype(BF16)
        dst[1, BLOCK:, :] = pltpu.roll(g0_lo, HEAD_DIM, 1).astype(BF16)
        dst[2, BLOCK:, :] = pltpu.roll(g1_hi, HEAD_DIM, 1).astype(BF16)
        dst[3, BLOCK:, :] = g1_hi.astype(BF16)

    sg_ref[...] = _silu(proj(OFF_AGATE, OFF_PIN))

    qi = lax.broadcasted_iota(jnp.int32, (BLOCK, 2 * BLOCK), 0)
    ki = lax.broadcasted_iota(jnp.int32, (BLOCK, 2 * BLOCK), 1)
    dist = qi + BLOCK - ki
    band = (dist >= 0) & (dist < WINDOW)
    band_first = band & (ki >= jnp.where(i == 0, BLOCK, 0))
    lane_b = lax.broadcasted_iota(jnp.int32, (BLOCK, LANES), 1)
    first_half_b = lane_b < HEAD_DIM

    for qb in range(nblk):
        r0 = qb * BLOCK
        mask = band_first if qb == 0 else band
        for g in range(N_KV_HEADS):
            c0 = g * LANE_TILES_PER_GROUP * LANES
            qs = jnp.concatenate(
                [q_ref[r0:r0 + BLOCK, c0 + j * LANES:c0 + (j + 1) * LANES]
                 for j in range(LANE_TILES_PER_GROUP)], axis=0)
            kcat = jnp.concatenate([kp_ref[2 * g, r0:r0 + 2 * BLOCK, :],
                                    kp_ref[2 * g + 1, r0:r0 + 2 * BLOCK, :]], axis=0)
            s = lax.dot_general(qs, kcat, (((1,), (1,)), ((), ())), preferred_element_type=F32)
            inv_l = {}
            for j in range(LANE_TILES_PER_GROUP):
                for par in range(HEADS_PER_LANE_TILE):
                    h = g * Q_GROUP + j * HEADS_PER_LANE_TILE + par
                    sh = s[j * BLOCK:(j + 1) * BLOCK, par * 2 * BLOCK:(par + 1) * 2 * BLOCK]
                    logits = jnp.where(mask, sh + bias_ref[h], NEG_INF)
                    sink = sinks_ref[h]
                    m = jnp.maximum(jnp.max(logits, axis=-1, keepdims=True), sink)
                    p = jnp.exp(logits - m)
                    l = jnp.sum(p, axis=-1, keepdims=True) + jnp.exp(sink - m)
                    p_ref[j * BLOCK:(j + 1) * BLOCK,
                          par * 2 * BLOCK:(par + 1) * 2 * BLOCK] = p.astype(BF16)
                    inv_l[(j, par)] = 1.0 / l
            vcat = jnp.concatenate([vp_ref[2 * g, r0:r0 + 2 * BLOCK, :],
                                    vp_ref[2 * g + 1, r0:r0 + 2 * BLOCK, :]], axis=0)
            o = jnp.dot(p_ref[...], vcat, preferred_element_type=F32)
            for j in range(LANE_TILES_PER_GROUP):
                cj = c0 + j * LANES
                norm = jnp.where(first_half_b, inv_l[(j, 0)], inv_l[(j, 1)])
                att = o[j * BLOCK:(j + 1) * BLOCK, :] * norm
                ain_ref[r0:r0 + BLOCK, cj:cj + LANES] = (
                    att * sg_ref[r0:r0 + BLOCK, cj:cj + LANES]).astype(BF16)

    kp_ref[:, 0:BLOCK, :] = kp_ref[:, tm:tm + BLOCK, :]
    vp_ref[:, 0:BLOCK, :] = vp_ref[:, tm:tm + BLOCK, :]

    a_proj = jnp.dot(ain_ref[...], w_ap_ref[...], preferred_element_type=F32)

    pin_ref[POOL_HALO:, :] = proj(OFF_PIN, OFF_PGATE)
    sp = _silu(proj(OFF_PGATE, OFF_MERGE)) * pscale_ref[...]
    tpos = (i * tm + lax.broadcasted_iota(jnp.int32, (tm, 1), 0) + 1).astype(F32)
    mixed = []
    for gi, w in enumerate(POOL_WINDOWS):
        c0 = gi * POOL_GROUP_WIDTH
        cur = pin_ref[POOL_HALO:, c0:c0 + POOL_GROUP_WIDTH]
        wsum = cur
        for dlt in range(1, w):
            wsum = wsum + pin_ref[POOL_HALO - dlt:POOL_HALO - dlt + tm, c0:c0 + POOL_GROUP_WIDTH]
        count = jnp.minimum(tpos, float(w))
        pooled = (wsum / count - cur).astype(BF16)
        mixed.append(jnp.dot(pooled, w_mix_ref[gi], preferred_element_type=F32))
    pin_ref[0:POOL_HALO, :] = pin_ref[tm:tm + POOL_HALO, :]
    pp = (jnp.concatenate(mixed, axis=1) * sp).astype(BF16)
    p_proj = jnp.dot(pp, w_pp_ref[...], preferred_element_type=F32)

    mg = proj(OFF_MERGE, IN_WIDTH)
    y_in = (_sigmoid(mg[:, 0:D_MODEL]) * a_proj + _sigmoid(mg[:, D_MODEL:]) * p_proj).astype(BF16)
    y = jnp.dot(y_in, w_out_ref[...], preferred_element_type=F32) * (1.0 + gate)
    r = DEEPNORM_ALPHA * x_ref[0] + y
    mu = jnp.mean(r, axis=-1, keepdims=True)
    rc = r - mu
    var = jnp.mean(rc * rc, axis=-1, keepdims=True)
    o_ref[0] = rc * lax.rsqrt(var + LN_EPS) * lng_ref[...] + lnb_ref[...]


def _layer(x, mod, sinks, bias, w_in, w_ap, w_pp, w_out, w_mix, pscale, lng, lnb, *, tm):
    batch, seq, _ = x.shape

    def const_spec(shape):
        zeros = (0,) * len(shape)
        return pl.BlockSpec(shape, lambda b, i: zeros, pipeline_mode=pl.Buffered(1))

    return pl.pallas_call(
        functools.partial(_layer_kernel, tm=tm),
        grid=(batch, seq // tm),
        in_specs=[
            pl.BlockSpec((1, tm, D_MODEL), lambda b, i: (b, i, 0)),
            pl.BlockSpec((1, 3, D_MODEL), lambda b, i: (b, 0, 0)),
            pl.BlockSpec(memory_space=pltpu.SMEM),
            const_spec((N_Q_HEADS, BLOCK, 2 * BLOCK)),
            const_spec((D_MODEL, IN_WIDTH)),
            const_spec((D_MODEL, D_MODEL)),
            const_spec((D_MODEL, D_MODEL)),
            const_spec((D_MODEL, D_MODEL)),
            const_spec((len(POOL_WINDOWS), POOL_GROUP_WIDTH, POOL_GROUP_WIDTH)),
            const_spec((1, D_MODEL)),
            const_spec((1, D_MODEL)),
            const_spec((1, D_MODEL)),
        ],
        out_specs=pl.BlockSpec((1, tm, D_MODEL), lambda b, i: (b, i, 0)),
        out_shape=jax.ShapeDtypeStruct(x.shape, F32),
        scratch_shapes=[
            pltpu.VMEM((tm, D_MODEL), BF16),
            pltpu.VMEM((tm, D_MODEL), BF16),
            pltpu.VMEM((4, tm + BLOCK, LANES), BF16),
            pltpu.VMEM((4, tm + BLOCK, LANES), BF16),
            pltpu.VMEM((LANE_TILES_PER_GROUP * BLOCK, HEADS_PER_LANE_TILE * 2 * BLOCK), BF16),
            pltpu.VMEM((tm, D_MODEL), F32),
            pltpu.VMEM((tm, D_MODEL), BF16),
            pltpu.VMEM((tm + POOL_HALO, D_MODEL), F32),
        ],
        compiler_params=pltpu.CompilerParams(
            dimension_semantics=("arbitrary", "arbitrary"),
            vmem_limit_bytes=VMEM_LIMIT_BYTES),
        name="fused_layer",
    )(x, mod, sinks, bias, w_in, w_ap, w_pp, w_out, w_mix, pscale, lng, lnb)


@functools.partial(jax.jit, static_argnames=("tm",))
def _forward(x, c, rel_bias, w_ada, b_ada, w_in, sinks, w_pool_mix, pool_scale,
             w_attn_proj, w_pool_proj, w_out, ln_gain, ln_bias, tm=256):
    batch = x.shape[0]
    mod = _adaln(c, w_ada, b_ada).reshape(DEPTH, batch, 3, D_MODEL)
    bias = _bias_table(rel_bias)
    for l in range(DEPTH):
        x = _layer(x, mod[l], sinks[l], bias,
                   w_in[l].astype(BF16), w_attn_proj[l].astype(BF16), w_pool_proj[l].astype(BF16),
                   w_out[l].astype(BF16), w_pool_mix[l].astype(BF16),
                   pool_scale[l].reshape(1, D_MODEL), ln_gain[l].reshape(1, D_MODEL),
                   ln_bias[l].reshape(1, D_MODEL), tm=tm)
    return x


def kernel(x, c, rel_bias, w_ada, b_ada, w_in, sinks, w_pool_mix, pool_scale, w_attn_proj,
           w_pool_proj, w_out, ln_gain, ln_bias):
    return _forward(x, c, rel_bias, w_ada, b_ada, w_in, sinks, w_pool_mix, pool_scale,
                    w_attn_proj, w_pool_proj, w_out, ln_gain, ln_bias)
```

```python
import functools
import math

import numpy as np
import jax
import jax.numpy as jnp
from jax import lax
from jax.experimental import pallas as pl
from jax.experimental.pallas import tpu as pltpu

D_MODEL = 1024
DEPTH = 2
N_Q_HEADS = 16
N_KV_HEADS = 2
HEAD_DIM = 64
Q_GROUP = N_Q_HEADS // N_KV_HEADS
WINDOW = 128
BLOCK = 128
NEG_INF = -1e30
NUM_BUCKETS = 32
MAX_DISTANCE = 128
POOL_WINDOWS = (2, 4, 8, 16)
POOL_GROUP_WIDTH = D_MODEL // len(POOL_WINDOWS)
POOL_HALO = 16
DEEPNORM_ALPHA = (2 * DEPTH) ** 0.25
LN_EPS = 1e-5

OFF_Q = 0
OFF_KV = 1024
OFF_AGATE = 1280
OFF_PIN = 2304
OFF_PGATE = 3328
OFF_MERGE = 4352
IN_WIDTH = 6400

LANES = 128
HEADS_PER_LANE_TILE = LANES // HEAD_DIM
LANE_TILES_PER_GROUP = Q_GROUP // HEADS_PER_LANE_TILE
VMEM_LIMIT_BYTES = 56 * 1024 * 1024

F32 = jnp.float32
BF16 = jnp.bfloat16


def _bucket_map():
    q = np.arange(BLOCK)[:, None]
    k = np.arange(2 * BLOCK)[None, :]
    dist = np.maximum(q + BLOCK - k, 0)
    max_exact = NUM_BUCKETS // 2
    d = np.maximum(dist, 1).astype(np.float64)
    large = max_exact + (np.log(d / max_exact) / math.log(MAX_DISTANCE / max_exact)
                         * (NUM_BUCKETS - max_exact)).astype(np.int32)
    large = np.minimum(large, NUM_BUCKETS - 1)
    return np.where(dist < max_exact, dist, large).astype(np.int32)


def _sigmoid(x):
    return 1.0 / (1.0 + jnp.exp(-x))


def _silu(x):
    return x * _sigmoid(x)


def _adaln_kernel(c_ref, w_ref, b_ref, o_ref):
    sc = _silu(c_ref[...])
    o_ref[0] = jnp.dot(sc, w_ref[0], preferred_element_type=F32,
                       precision=lax.Precision.HIGHEST) + b_ref[0]


def _adaln(c, w_ada, b_ada):
    batch = c.shape[0]
    tn = 1024
    n_out = w_ada.shape[-1]
    return pl.pallas_call(
        _adaln_kernel,
        grid=(DEPTH, n_out // tn),
        in_specs=[
            pl.BlockSpec((batch, D_MODEL), lambda l, n: (0, 0)),
            pl.BlockSpec((1, D_MODEL, tn), lambda l, n: (l, 0, n)),
            pl.BlockSpec((1, 1, tn), lambda l, n: (l, 0, n)),
        ],
        out_specs=pl.BlockSpec((1, batch, tn), lambda l, n: (l, 0, n)),
        out_shape=jax.ShapeDtypeStruct((DEPTH, batch, n_out), F32),
        name="adaln_mod",
    )(c, w_ada, b_ada.reshape(DEPTH, 1, n_out))


def _bias_kernel(rel_ref, bucket_ref, o_ref):
    h = pl.program_id(0)
    bucket = bucket_ref[...]
    acc = jnp.zeros(bucket.shape, F32)
    for b in range(NUM_BUCKETS):
        acc = jnp.where(bucket == b, rel_ref[b, h], acc)
    o_ref[0] = acc


def _bias_table(rel_bias):
    bucket = jnp.asarray(_bucket_map())
    return pl.pallas_call(
        _bias_kernel,
        grid=(N_Q_HEADS,),
        in_specs=[
            pl.BlockSpec(memory_space=pltpu.SMEM),
            pl.BlockSpec((BLOCK, 2 * BLOCK), lambda h: (0, 0)),
        ],
        out_specs=pl.BlockSpec((1, BLOCK, 2 * BLOCK), lambda h: (h, 0, 0)),
        out_shape=jax.ShapeDtypeStruct((N_Q_HEADS, BLOCK, 2 * BLOCK), F32),
        name="t5_bias_table",
    )(rel_bias, bucket)


def _layer_kernel(x_ref, mod_ref, sinks_ref, bias_ref, bsum_ref, bhalo_ref, w_in_ref, w_ap_ref,
                  w_pp_ref, w_out_ref, w_mix_ref, pscale_ref, lng_ref, lnb_ref, o_ref,
                  u_ref, q_ref, kp_ref, vp_ref, p_ref, sg_ref, ain_ref, halo_ref, h_ref, *, tm):
    i = pl.program_id(1)
    nblk = tm // BLOCK

    @pl.when(i == 0)
    def _():
        kp_ref[:, 0:BLOCK, :] = jnp.zeros((4, BLOCK, LANES), BF16)
        vp_ref[:, 0:BLOCK, :] = jnp.zeros((4, BLOCK, LANES), BF16)
        halo_ref[...] = jnp.zeros((POOL_HALO, D_MODEL), BF16)

    x = x_ref[0]
    shift = mod_ref[0, 0:1, :]
    scale = mod_ref[0, 1:2, :]
    gate = mod_ref[0, 2:3, :]
    u_ref[...] = (x * (1.0 + scale) + shift).astype(BF16)

    def proj(lo, hi):
        return jnp.dot(u_ref[...], w_in_ref[0, :, lo:hi], preferred_element_type=F32)

    qkv = proj(OFF_Q, OFF_AGATE)
    q_ref[...] = (qkv[:, 0:OFF_KV] * (HEAD_DIM ** -0.5)).astype(BF16)
    lane = lax.broadcasted_iota(jnp.int32, (tm, LANES), 1)
    first_half = lane < HEAD_DIM
    for src, dst in ((qkv[:, OFF_KV:OFF_KV + LANES], kp_ref),
                     (qkv[:, OFF_KV + LANES:OFF_KV + 2 * LANES], vp_ref)):
        g0_lo = jnp.where(first_half, src, 0.0)
        g1_hi = jnp.where(first_half, 0.0, src)
        dst[0, BLOCK:, :] = g0_lo.astype(BF16)
        dst[1, BLOCK:, :] = pltpu.roll(g0_lo, HEAD_DIM, 1).astype(BF16)
        dst[2, BLOCK:, :] = pltpu.roll(g1_hi, HEAD_DIM, 1).astype(BF16)
        dst[3, BLOCK:, :] = g1_hi.astype(BF16)

    sg_ref[...] = _silu(proj(OFF_AGATE, OFF_PIN))

    qi = lax.broadcasted_iota(jnp.int32, (BLOCK, 2 * BLOCK), 0)
    ki = lax.broadcasted_iota(jnp.int32, (BLOCK, 2 * BLOCK), 1)
    dist = qi + BLOCK - ki
    band = (dist >= 0) & (dist < WINDOW)
    band_first = band & (ki >= jnp.where(i == 0, BLOCK, 0))
    lane_b = lax.broadcasted_iota(jnp.int32, (BLOCK, LANES), 1)
    first_half_b = lane_b < HEAD_DIM
    cw = (IN_WIDTH - OFF_PIN) // (nblk * N_KV_HEADS)

    for qb in range(nblk):
        r0 = qb * BLOCK
        mask = band_first if qb == 0 else band
        for g in range(N_KV_HEADS):
            c0 = g * LANE_TILES_PER_GROUP * LANES
            qs = jnp.concatenate(
                [q_ref[r0:r0 + BLOCK, c0 + j * LANES:c0 + (j + 1) * LANES]
                 for j in range(LANE_TILES_PER_GROUP)], axis=0)
            kcat = jnp.concatenate([kp_ref[2 * g, r0:r0 + 2 * BLOCK, :],
                                    kp_ref[2 * g + 1, r0:r0 + 2 * BLOCK, :]], axis=0)
            s = lax.dot_general(qs, kcat, (((1,), (1,)), ((), ())), preferred_element_type=F32)
            it = qb * N_KV_HEADS + g
            h_ref[:, it * cw:(it + 1) * cw] = proj(OFF_PIN + it * cw, OFF_PIN + (it + 1) * cw)
            inv_l = {}
            for j in range(LANE_TILES_PER_GROUP):
                for par in range(HEADS_PER_LANE_TILE):
                    h = g * Q_GROUP + j * HEADS_PER_LANE_TILE + par
                    sh = s[j * BLOCK:(j + 1) * BLOCK, par * 2 * BLOCK:(par + 1) * 2 * BLOCK]
                    logits = jnp.where(mask, sh + bias_ref[h], NEG_INF)
                    sink = sinks_ref[h]
                    m = jnp.maximum(jnp.max(logits, axis=-1, keepdims=True), sink)
                    p = jnp.exp(logits - m)
                    l = jnp.sum(p, axis=-1, keepdims=True) + jnp.exp(sink - m)
                    p_ref[j * BLOCK:(j + 1) * BLOCK,
                          par * 2 * BLOCK:(par + 1) * 2 * BLOCK] = p.astype(BF16)
                    inv_l[(j, par)] = 1.0 / l
            vcat = jnp.concatenate([vp_ref[2 * g, r0:r0 + 2 * BLOCK, :],
                                    vp_ref[2 * g + 1, r0:r0 + 2 * BLOCK, :]], axis=0)
            o = jnp.dot(p_ref[...], vcat, preferred_element_type=F32)
            for j in range(LANE_TILES_PER_GROUP):
                cj = c0 + j * LANES
                norm = jnp.where(first_half_b, inv_l[(j, 0)], inv_l[(j, 1)])
                att = o[j * BLOCK:(j + 1) * BLOCK, :] * norm
                ain_ref[r0:r0 + BLOCK, cj:cj + LANES] = (
                    att * sg_ref[r0:r0 + BLOCK, cj:cj + LANES]).astype(BF16)

    kp_ref[:, 0:BLOCK, :] = kp_ref[:, tm:tm + BLOCK, :]
    vp_ref[:, 0:BLOCK, :] = vp_ref[:, tm:tm + BLOCK, :]

    a_proj = jnp.dot(ain_ref[...], w_ap_ref[0], preferred_element_type=F32)

    pin = h_ref[:, 0:D_MODEL]
    pin_b = pin.astype(BF16)
    sp = _silu(h_ref[:, D_MODEL:2 * D_MODEL]) * pscale_ref[0]
    tpos = (i * tm + lax.broadcasted_iota(jnp.int32, (tm, 1), 0) + 1).astype(F32)
    mixed = []
    for gi, w in enumerate(POOL_WINDOWS):
        cs = slice(gi * POOL_GROUP_WIDTH, (gi + 1) * POOL_GROUP_WIDTH)
        wsum = jnp.dot(bsum_ref[gi], pin_b[:, cs], preferred_element_type=F32)
        head = wsum[0:POOL_HALO] + jnp.dot(bhalo_ref[gi], halo_ref[:, cs],
                                           preferred_element_type=F32)
        wsum = jnp.concatenate([head, wsum[POOL_HALO:]], axis=0)
        count = jnp.minimum(tpos, float(w))
        pooled = (wsum / count - pin[:, cs]).astype(BF16)
        mixed.append(jnp.dot(pooled, w_mix_ref[0, gi], preferred_element_type=F32))
    halo_ref[...] = pin_b[tm - POOL_HALO:tm, :]
    pp = (jnp.concatenate(mixed, axis=1) * sp).astype(BF16)
    p_proj = jnp.dot(pp, w_pp_ref[0], preferred_element_type=F32)

    y_in = (_sigmoid(h_ref[:, 2 * D_MODEL:3 * D_MODEL]) * a_proj
            + _sigmoid(h_ref[:, 3 * D_MODEL:4 * D_MODEL]) * p_proj).astype(BF16)
    y = jnp.dot(y_in, w_out_ref[0], preferred_element_type=F32) * (1.0 + gate)
    r = DEEPNORM_ALPHA * x_ref[0] + y
    mu = jnp.mean(r, axis=-1, keepdims=True)
    rc = r - mu
    var = jnp.mean(rc * rc, axis=-1, keepdims=True)
    o_ref[0] = rc * lax.rsqrt(var + LN_EPS) * lng_ref[0] + lnb_ref[0]


def _pool_band_matrices(tm):
    t = np.arange(tm)[:, None]
    own = np.stack([((t - np.arange(tm)[None, :] >= 0) & (t - np.arange(tm)[None, :] < w))
                    for w in POOL_WINDOWS])
    th = np.arange(POOL_HALO)[:, None]
    back = th - (np.arange(POOL_HALO)[None, :] - POOL_HALO)
    halo = np.stack([back < w for w in POOL_WINDOWS])
    return own.astype(np.float32), halo.astype(np.float32)


def _layer(layer, x, mod, sinks, bias, w_in, w_ap, w_pp, w_out, w_mix, pscale, lng, lnb, *, tm):
    batch, seq, _ = x.shape
    n_groups = len(POOL_WINDOWS)
    bsum, bhalo = _pool_band_matrices(tm)

    def const_spec(shape):
        zeros = (0,) * len(shape)
        return pl.BlockSpec(shape, lambda b, i: zeros, pipeline_mode=pl.Buffered(1))

    def layer_spec(shape):
        idx = (layer,) + (0,) * len(shape)
        return pl.BlockSpec((1,) + shape, lambda b, i: idx, pipeline_mode=pl.Buffered(1))

    return pl.pallas_call(
        functools.partial(_layer_kernel, tm=tm),
        grid=(batch, seq // tm),
        in_specs=[
            pl.BlockSpec((1, tm, D_MODEL), lambda b, i: (b, i, 0)),
            pl.BlockSpec((1, 3, D_MODEL), lambda b, i: (b, 0, 0)),
            pl.BlockSpec(memory_space=pltpu.SMEM),
            const_spec((N_Q_HEADS, BLOCK, 2 * BLOCK)),
            const_spec((n_groups, tm, tm)),
            const_spec((n_groups, POOL_HALO, POOL_HALO)),
            layer_spec((D_MODEL, IN_WIDTH)),
            layer_spec((D_MODEL, D_MODEL)),
            layer_spec((D_MODEL, D_MODEL)),
            layer_spec((D_MODEL, D_MODEL)),
            layer_spec((n_groups, POOL_GROUP_WIDTH, POOL_GROUP_WIDTH)),
            layer_spec((1, D_MODEL)),
            layer_spec((1, D_MODEL)),
            layer_spec((1, D_MODEL)),
        ],
        out_specs=pl.BlockSpec((1, tm, D_MODEL), lambda b, i: (b, i, 0)),
        out_shape=jax.ShapeDtypeStruct(x.shape, F32),
        scratch_shapes=[
            pltpu.VMEM((tm, D_MODEL), BF16),
            pltpu.VMEM((tm, D_MODEL), BF16),
            pltpu.VMEM((4, tm + BLOCK, LANES), BF16),
            pltpu.VMEM((4, tm + BLOCK, LANES), BF16),
            pltpu.VMEM((LANE_TILES_PER_GROUP * BLOCK, HEADS_PER_LANE_TILE * 2 * BLOCK), BF16),
            pltpu.VMEM((tm, D_MODEL), F32),
            pltpu.VMEM((tm, D_MODEL), BF16),
            pltpu.VMEM((POOL_HALO, D_MODEL), BF16),
            pltpu.VMEM((tm, IN_WIDTH - OFF_PIN), F32),
        ],
        compiler_params=pltpu.CompilerParams(
            dimension_semantics=("arbitrary", "arbitrary"),
            vmem_limit_bytes=VMEM_LIMIT_BYTES),
        name="fused_layer",
    )(x, mod, sinks, bias, jnp.asarray(bsum, BF16), jnp.asarray(bhalo, BF16),
      w_in, w_ap, w_pp, w_out, w_mix, pscale, lng, lnb)


@functools.partial(jax.jit, static_argnames=("tm",))
def _forward(x, c, rel_bias, w_ada, b_ada, w_in, sinks, w_pool_mix, pool_scale,
             w_attn_proj, w_pool_proj, w_out, ln_gain, ln_bias, tm=256):
    batch = x.shape[0]
    mod = _adaln(c, w_ada, b_ada).reshape(DEPTH, batch, 3, D_MODEL)
    bias = _bias_table(rel_bias)
    w_in, w_ap, w_pp, w_o, w_mix = (w.astype(BF16) for w in
                                    (w_in, w_attn_proj, w_pool_proj, w_out, w_pool_mix))
    pscale, lng, lnb = (v.reshape(DEPTH, 1, D_MODEL) for v in (pool_scale, ln_gain, ln_bias))
    for l in range(DEPTH):
        x = _layer(l, x, mod[l], sinks[l], bias, w_in, w_ap, w_pp, w_o, w_mix,
                   pscale, lng, lnb, tm=tm)
    return x


def kernel(x, c, rel_bias, w_ada, b_ada, w_in, sinks, w_pool_mix, pool_scale, w_attn_proj,
           w_pool_proj, w_out, ln_gain, ln_bias):
    return _forward(x, c, rel_bias, w_ada, b_ada, w_in, sinks, w_pool_mix, pool_scale,
                    w_attn_proj, w_pool_proj, w_out, ln_gain, ln_bias)
```

```python
import functools
import math

import numpy as np
import jax
import jax.numpy as jnp
from jax import lax
from jax.experimental import pallas as pl
from jax.experimental.pallas import tpu as pltpu

D_MODEL = 1024
DEPTH = 2
N_Q_HEADS = 16
N_KV_HEADS = 2
HEAD_DIM = 64
Q_GROUP = N_Q_HEADS // N_KV_HEADS
WINDOW = 128
BLOCK = 128
NEG_INF = -1e30
NUM_BUCKETS = 32
MAX_DISTANCE = 128
POOL_WINDOWS = (2, 4, 8, 16)
POOL_GROUP_WIDTH = D_MODEL // len(POOL_WINDOWS)
POOL_HALO = 16
DEEPNORM_ALPHA = (2 * DEPTH) ** 0.25
LN_EPS = 1e-5

OFF_Q = 0
OFF_KV = 1024
OFF_AGATE = 1280
OFF_PIN = 2304
OFF_PGATE = 3328
OFF_MERGE = 4352
IN_WIDTH = 6400

LANES = 128
HEADS_PER_LANE_TILE = LANES // HEAD_DIM
LANE_TILES_PER_GROUP = Q_GROUP // HEADS_PER_LANE_TILE
VMEM_LIMIT_BYTES = 56 * 1024 * 1024

F32 = jnp.float32
BF16 = jnp.bfloat16


def _bucket_map():
    q = np.arange(BLOCK)[:, None]
    k = np.arange(2 * BLOCK)[None, :]
    dist = np.maximum(q + BLOCK - k, 0)
    max_exact = NUM_BUCKETS // 2
    d = np.maximum(dist, 1).astype(np.float64)
    large = max_exact + (np.log(d / max_exact) / math.log(MAX_DISTANCE / max_exact)
                         * (NUM_BUCKETS - max_exact)).astype(np.int32)
    large = np.minimum(large, NUM_BUCKETS - 1)
    return np.where(dist < max_exact, dist, large).astype(np.int32)


def _sigmoid(x):
    return 1.0 / (1.0 + jnp.exp(-x))


def _silu(x):
    return x * _sigmoid(x)


def _adaln_kernel(c_ref, w_ref, b_ref, o_ref):
    sc = _silu(c_ref[...])
    o_ref[0] = jnp.dot(sc, w_ref[0], preferred_element_type=F32,
                       precision=lax.Precision.HIGHEST) + b_ref[0]


def _adaln(c, w_ada, b_ada):
    batch = c.shape[0]
    tn = 1024
    n_out = w_ada.shape[-1]
    return pl.pallas_call(
        _adaln_kernel,
        grid=(DEPTH, n_out // tn),
        in_specs=[
            pl.BlockSpec((batch, D_MODEL), lambda l, n: (0, 0)),
            pl.BlockSpec((1, D_MODEL, tn), lambda l, n: (l, 0, n)),
            pl.BlockSpec((1, 1, tn), lambda l, n: (l, 0, n)),
        ],
        out_specs=pl.BlockSpec((1, batch, tn), lambda l, n: (l, 0, n)),
        out_shape=jax.ShapeDtypeStruct((DEPTH, batch, n_out), F32),
        name="adaln_mod",
    )(c, w_ada, b_ada.reshape(DEPTH, 1, n_out))


def _bias_kernel(rel_ref, bucket_ref, o_ref):
    h = pl.program_id(0)
    bucket = bucket_ref[...]
    acc = jnp.zeros(bucket.shape, F32)
    for b in range(NUM_BUCKETS):
        acc = jnp.where(bucket == b, rel_ref[b, h], acc)
    o_ref[0] = acc


def _bias_table(rel_bias):
    bucket = jnp.asarray(_bucket_map())
    return pl.pallas_call(
        _bias_kernel,
        grid=(N_Q_HEADS,),
        in_specs=[
            pl.BlockSpec(memory_space=pltpu.SMEM),
            pl.BlockSpec((BLOCK, 2 * BLOCK), lambda h: (0, 0)),
        ],
        out_specs=pl.BlockSpec((1, BLOCK, 2 * BLOCK), lambda h: (h, 0, 0)),
        out_shape=jax.ShapeDtypeStruct((N_Q_HEADS, BLOCK, 2 * BLOCK), F32),
        name="t5_bias_table",
    )(rel_bias, bucket)


def _layer_kernel(x_ref, mod_ref, sinks_ref, bias_ref, bsum_ref, bhalo_ref, w_in_ref, w_ap_ref,
                  w_pp_ref, w_out_ref, w_mix_ref, pscale_ref, lng_ref, lnb_ref, o_ref,
                  u_ref, q_ref, kp_ref, vp_ref, p_ref, sg_ref, ain_ref, halo_ref, h_ref, *, tm):
    i = pl.program_id(1)
    nblk = tm // BLOCK

    @pl.when(i == 0)
    def _():
        kp_ref[:, 0:BLOCK, :] = jnp.zeros((4, BLOCK, LANES), BF16)
        vp_ref[:, 0:BLOCK, :] = jnp.zeros((4, BLOCK, LANES), BF16)
        halo_ref[...] = jnp.zeros((POOL_HALO, D_MODEL), BF16)

    x = x_ref[0]
    shift = mod_ref[0, 0:1, :]
    scale = mod_ref[0, 1:2, :]
    gate = mod_ref[0, 2:3, :]
    u_ref[...] = (x * (1.0 + scale) + shift).astype(BF16)

    def proj(lo, hi):
        return jnp.dot(u_ref[...], w_in_ref[0, :, lo:hi], preferred_element_type=F32)

    qkv = proj(OFF_Q, OFF_AGATE)
    q_ref[...] = (qkv[:, 0:OFF_KV] * (HEAD_DIM ** -0.5)).astype(BF16)
    lane = lax.broadcasted_iota(jnp.int32, (tm, LANES), 1)
    first_half = lane < HEAD_DIM
    for src, dst in ((qkv[:, OFF_KV:OFF_KV + LANES], kp_ref),
                     (qkv[:, OFF_KV + LANES:OFF_KV + 2 * LANES], vp_ref)):
        g0_lo = jnp.where(first_half, src, 0.0)
        g1_hi = jnp.where(first_half, 0.0, src)
        dst[0, BLOCK:, :] = g0_lo.astype(BF16)
        dst[1, BLOCK:, :] = pltpu.roll(g0_lo, HEAD_DIM, 1).astype(BF16)
        dst[2, BLOCK:, :] = pltpu.roll(g1_hi, HEAD_DIM, 1).astype(BF16)
        dst[3, BLOCK:, :] = g1_hi.astype(BF16)

    sg_ref[...] = _silu(proj(OFF_AGATE, OFF_PIN))

    qi = lax.broadcasted_iota(jnp.int32, (BLOCK, 2 * BLOCK), 0)
    ki = lax.broadcasted_iota(jnp.int32, (BLOCK, 2 * BLOCK), 1)
    dist = qi + BLOCK - ki
    band = (dist >= 0) & (dist < WINDOW)
    band_first = band & (ki >= jnp.where(i == 0, BLOCK, 0))
    lane_b = lax.broadcasted_iota(jnp.int32, (BLOCK, LANES), 1)
    first_half_b = lane_b < HEAD_DIM
    cw = (IN_WIDTH - OFF_PIN) // (nblk * N_KV_HEADS)

    for qb in range(nblk):
        r0 = qb * BLOCK
        mask = band_first if qb == 0 else band
        for g in range(N_KV_HEADS):
            c0 = g * LANE_TILES_PER_GROUP * LANES
            qs = jnp.concatenate(
                [q_ref[r0:r0 + BLOCK, c0 + j * LANES:c0 + (j + 1) * LANES]
                 for j in range(LANE_TILES_PER_GROUP)], axis=0)
            kcat = jnp.concatenate([kp_ref[2 * g, r0:r0 + 2 * BLOCK, :],
                                    kp_ref[2 * g + 1, r0:r0 + 2 * BLOCK, :]], axis=0)
            s = lax.dot_general(qs, kcat, (((1,), (1,)), ((), ())), preferred_element_type=F32)
            it = qb * N_KV_HEADS + g
            h_ref[:, it * cw:(it + 1) * cw] = proj(OFF_PIN + it * cw, OFF_PIN + (it + 1) * cw)
            inv_l = {}
            for j in range(LANE_TILES_PER_GROUP):
                for par in range(HEADS_PER_LANE_TILE):
                    h = g * Q_GROUP + j * HEADS_PER_LANE_TILE + par
                    sh = s[j * BLOCK:(j + 1) * BLOCK, par * 2 * BLOCK:(par + 1) * 2 * BLOCK]
                    logits = jnp.where(mask, sh + bias_ref[h], NEG_INF)
                    sink = sinks_ref[h]
                    m = jnp.maximum(jnp.max(logits, axis=-1, keepdims=True), sink)
                    p = jnp.exp(logits - m)
                    l = jnp.sum(p, axis=-1, keepdims=True) + jnp.exp(sink - m)
                    p_ref[j * BLOCK:(j + 1) * BLOCK,
                          par * 2 * BLOCK:(par + 1) * 2 * BLOCK] = p.astype(BF16)
                    inv_l[(j, par)] = 1.0 / l
            vcat = jnp.concatenate([vp_ref[2 * g, r0:r0 + 2 * BLOCK, :],
                                    vp_ref[2 * g + 1, r0:r0 + 2 * BLOCK, :]], axis=0)
            o = jnp.dot(p_ref[...], vcat, preferred_element_type=F32)
            for j in range(LANE_TILES_PER_GROUP):
                cj = c0 + j * LANES
                norm = jnp.where(first_half_b, inv_l[(j, 0)], inv_l[(j, 1)])
                att = o[j * BLOCK:(j + 1) * BLOCK, :] * norm
                ain_ref[r0:r0 + BLOCK, cj:cj + LANES] = (
                    att * sg_ref[r0:r0 + BLOCK, cj:cj + LANES]).astype(BF16)

    kp_ref[:, 0:BLOCK, :] = kp_ref[:, tm:tm + BLOCK, :]
    vp_ref[:, 0:BLOCK, :] = vp_ref[:, tm:tm + BLOCK, :]

    a_proj = jnp.dot(ain_ref[...], w_ap_ref[0], preferred_element_type=F32)

    pin = h_ref[:, 0:D_MODEL]
    pin_b = pin.astype(BF16)
    sp = _silu(h_ref[:, D_MODEL:2 * D_MODEL]) * pscale_ref[0]
    tpos = (i * tm + lax.broadcasted_iota(jnp.int32, (tm, 1), 0) + 1).astype(F32)
    mixed = []
    for gi, w in enumerate(POOL_WINDOWS):
        cs = slice(gi * POOL_GROUP_WIDTH, (gi + 1) * POOL_GROUP_WIDTH)
        wsum = jnp.dot(bsum_ref[gi], pin_b[:, cs], preferred_element_type=F32)
        head = wsum[0:POOL_HALO] + jnp.dot(bhalo_ref[gi], halo_ref[:, cs],
                                           preferred_element_type=F32)
        wsum = jnp.concatenate([head, wsum[POOL_HALO:]], axis=0)
        count = jnp.minimum(tpos, float(w))
        pooled = (wsum / count - pin[:, cs]).astype(BF16)
        mixed.append(jnp.dot(pooled, w_mix_ref[0, gi], preferred_element_type=F32))
    halo_ref[...] = pin_b[tm - POOL_HALO:tm, :]
    pp = (jnp.concatenate(mixed, axis=1) * sp).astype(BF16)
    p_proj = jnp.dot(pp, w_pp_ref[0], preferred_element_type=F32)

    y_in = (_sigmoid(h_ref[:, 2 * D_MODEL:3 * D_MODEL]) * a_proj
            + _sigmoid(h_ref[:, 3 * D_MODEL:4 * D_MODEL]) * p_proj).astype(BF16)
    y = jnp.dot(y_in, w_out_ref[0], preferred_element_type=F32) * (1.0 + gate)
    r = DEEPNORM_ALPHA * x_ref[0] + y
    mu = jnp.mean(r, axis=-1, keepdims=True)
    rc = r - mu
    var = jnp.mean(rc * rc, axis=-1, keepdims=True)
    o_ref[0] = rc * lax.rsqrt(var + LN_EPS) * lng_ref[0] + lnb_ref[0]


def _pool_band_matrices(tm):
    t = np.arange(tm)[:, None]
    own = np.stack([((t - np.arange(tm)[None, :] >= 0) & (t - np.arange(tm)[None, :] < w))
                    for w in POOL_WINDOWS])
    th = np.arange(POOL_HALO)[:, None]
    back = th - (np.arange(POOL_HALO)[None, :] - POOL_HALO)
    halo = np.stack([back < w for w in POOL_WINDOWS])
    return own.astype(np.float32), halo.astype(np.float32)


def _layer(layer, x, mod, sinks, bias, w_in, w_ap, w_pp, w_out, w_mix, pscale, lng, lnb, *, tm):
    batch, seq, _ = x.shape
    n_groups = len(POOL_WINDOWS)
    bsum, bhalo = _pool_band_matrices(tm)

    def const_spec(shape):
        zeros = (0,) * len(shape)
        return pl.BlockSpec(shape, lambda b, i: zeros, pipeline_mode=pl.Buffered(1))

    def layer_spec(shape):
        idx = (layer,) + (0,) * len(shape)
        return pl.BlockSpec((1,) + shape, lambda b, i: idx, pipeline_mode=pl.Buffered(1))

    return pl.pallas_call(
        functools.partial(_layer_kernel, tm=tm),
        grid=(batch, seq // tm),
        in_specs=[
            pl.BlockSpec((1, tm, D_MODEL), lambda b, i: (b, i, 0)),
            pl.BlockSpec((1, 3, D_MODEL), lambda b, i: (b, 0, 0)),
            pl.BlockSpec(memory_space=pltpu.SMEM),
            const_spec((N_Q_HEADS, BLOCK, 2 * BLOCK)),
            const_spec((n_groups, tm, tm)),
            const_spec((n_groups, POOL_HALO, POOL_HALO)),
            layer_spec((D_MODEL, IN_WIDTH)),
            layer_spec((D_MODEL, D_MODEL)),
            layer_spec((D_MODEL, D_MODEL)),
            layer_spec((D_MODEL, D_MODEL)),
            layer_spec((n_groups, POOL_GROUP_WIDTH, POOL_GROUP_WIDTH)),
            layer_spec((1, D_MODEL)),
            layer_spec((1, D_MODEL)),
            layer_spec((1, D_MODEL)),
        ],
        out_specs=pl.BlockSpec((1, tm, D_MODEL), lambda b, i: (b, i, 0)),
        out_shape=jax.ShapeDtypeStruct(x.shape, F32),
        scratch_shapes=[
            pltpu.VMEM((tm, D_MODEL), BF16),
            pltpu.VMEM((tm, D_MODEL), BF16),
            pltpu.VMEM((4, tm + BLOCK, LANES), BF16),
            pltpu.VMEM((4, tm + BLOCK, LANES), BF16),
            pltpu.VMEM((LANE_TILES_PER_GROUP * BLOCK, HEADS_PER_LANE_TILE * 2 * BLOCK), BF16),
            pltpu.VMEM((tm, D_MODEL), F32),
            pltpu.VMEM((tm, D_MODEL), BF16),
            pltpu.VMEM((POOL_HALO, D_MODEL), BF16),
            pltpu.VMEM((tm, IN_WIDTH - OFF_PIN), F32),
        ],
        compiler_params=pltpu.CompilerParams(
            dimension_semantics=("arbitrary", "arbitrary"),
            vmem_limit_bytes=VMEM_LIMIT_BYTES),
        name="fused_layer",
    )(x, mod, sinks, bias, jnp.asarray(bsum, BF16), jnp.asarray(bhalo, BF16),
      w_in, w_ap, w_pp, w_out, w_mix, pscale, lng, lnb)


@functools.partial(jax.jit, static_argnames=("tm",))
def _forward(x, c, rel_bias, w_ada, b_ada, w_in, sinks, w_pool_mix, pool_scale,
             w_attn_proj, w_pool_proj, w_out, ln_gain, ln_bias, tm=512):
    batch = x.shape[0]
    mod = _adaln(c, w_ada, b_ada).reshape(DEPTH, batch, 3, D_MODEL)
    bias = _bias_table(rel_bias)
    w_in, w_ap, w_pp, w_o, w_mix = (w.astype(BF16) for w in
                                    (w_in, w_attn_proj, w_pool_proj, w_out, w_pool_mix))
    pscale, lng, lnb = (v.reshape(DEPTH, 1, D_MODEL) for v in (pool_scale, ln_gain, ln_bias))
    for l in range(DEPTH):
        x = _layer(l, x, mod[l], sinks[l], bias, w_in, w_ap, w_pp, w_o, w_mix,
                   pscale, lng, lnb, tm=tm)
    return x


def kernel(x, c, rel_bias, w_ada, b_ada, w_in, sinks, w_pool_mix, pool_scale, w_attn_proj,
           w_pool_proj, w_out, ln_gain, ln_bias):
    return _forward(x, c, rel_bias, w_ada, b_ada, w_in, sinks, w_pool_mix, pool_scale,
                    w_attn_proj, w_pool_proj, w_out, ln_gain, ln_bias)
```

```python
import functools
import math

import numpy as np
import jax
import jax.numpy as jnp
from jax import lax
from jax.experimental import pallas as pl
from jax.experimental.pallas import tpu as pltpu

D_MODEL = 1024
DEPTH = 2
N_Q_HEADS = 16
N_KV_HEADS = 2
HEAD_DIM = 64
Q_GROUP = N_Q_HEADS // N_KV_HEADS
WINDOW = 128
BLOCK = 128
NEG_INF = -1e30
NUM_BUCKETS = 32
MAX_DISTANCE = 128
POOL_WINDOWS = (2, 4, 8, 16)
POOL_GROUP_WIDTH = D_MODEL // len(POOL_WINDOWS)
POOL_HALO = 16
DEEPNORM_ALPHA = (2 * DEPTH) ** 0.25
LN_EPS = 1e-5
LOG2E = math.log2(math.e)
POOL_ROWS = 256

OFF_Q = 0
OFF_KV = 1024
OFF_AGATE = 1280
OFF_PIN = 2304
OFF_PGATE = 3328
OFF_MERGE = 4352
IN_WIDTH = 6400

LANES = 128
HEADS_PER_LANE_TILE = LANES // HEAD_DIM
LANE_TILES_PER_GROUP = Q_GROUP // HEADS_PER_LANE_TILE
VMEM_LIMIT_BYTES = 56 * 1024 * 1024

F32 = jnp.float32
BF16 = jnp.bfloat16


def _bucket_map():
    q = np.arange(BLOCK)[:, None]
    k = np.arange(2 * BLOCK)[None, :]
    dist = np.maximum(q + BLOCK - k, 0)
    max_exact = NUM_BUCKETS // 2
    d = np.maximum(dist, 1).astype(np.float64)
    large = max_exact + (np.log(d / max_exact) / math.log(MAX_DISTANCE / max_exact)
                         * (NUM_BUCKETS - max_exact)).astype(np.int32)
    large = np.minimum(large, NUM_BUCKETS - 1)
    return np.where(dist < max_exact, dist, large).astype(np.int32)


def _sigmoid(x):
    return 1.0 / (1.0 + jnp.exp(-x))


def _silu(x):
    return x * _sigmoid(x)


def _adaln_kernel(c_ref, w_ref, b_ref, o_ref):
    sc = _silu(c_ref[...])
    o_ref[0] = jnp.dot(sc, w_ref[0], preferred_element_type=F32,
                       precision=lax.Precision.HIGHEST) + b_ref[0]


def _adaln(c, w_ada, b_ada):
    batch = c.shape[0]
    tn = 1024
    n_out = w_ada.shape[-1]
    return pl.pallas_call(
        _adaln_kernel,
        grid=(DEPTH, n_out // tn),
        in_specs=[
            pl.BlockSpec((batch, D_MODEL), lambda l, n: (0, 0)),
            pl.BlockSpec((1, D_MODEL, tn), lambda l, n: (l, 0, n)),
            pl.BlockSpec((1, 1, tn), lambda l, n: (l, 0, n)),
        ],
        out_specs=pl.BlockSpec((1, batch, tn), lambda l, n: (l, 0, n)),
        out_shape=jax.ShapeDtypeStruct((DEPTH, batch, n_out), F32),
        name="adaln_mod",
    )(c, w_ada, b_ada.reshape(DEPTH, 1, n_out))


BIAS_HEADS_PER_STEP = 8
BIAS_ROW_CHUNK = 32


def _bias_kernel(rel_ref, bucket_ref, o_ref):
    h0 = pl.program_id(0) * BIAS_HEADS_PER_STEP
    for r0 in range(0, BLOCK, BIAS_ROW_CHUNK):
        bucket = bucket_ref[r0:r0 + BIAS_ROW_CHUNK, :]
        acc = [jnp.zeros(bucket.shape, F32) for _ in range(BIAS_HEADS_PER_STEP)]
        for b in range(NUM_BUCKETS):
            hit = bucket == b
            for hh in range(BIAS_HEADS_PER_STEP):
                acc[hh] = jnp.where(hit, rel_ref[b, h0 + hh] * LOG2E, acc[hh])
        for hh in range(BIAS_HEADS_PER_STEP):
            o_ref[hh, r0:r0 + BIAS_ROW_CHUNK, :] = acc[hh]


def _bias_table(rel_bias):
    bucket = jnp.asarray(_bucket_map())
    return pl.pallas_call(
        _bias_kernel,
        grid=(N_Q_HEADS // BIAS_HEADS_PER_STEP,),
        in_specs=[
            pl.BlockSpec(memory_space=pltpu.SMEM),
            pl.BlockSpec((BLOCK, 2 * BLOCK), lambda h: (0, 0)),
        ],
        out_specs=pl.BlockSpec((BIAS_HEADS_PER_STEP, BLOCK, 2 * BLOCK), lambda h: (h, 0, 0)),
        out_shape=jax.ShapeDtypeStruct((N_Q_HEADS, BLOCK, 2 * BLOCK), F32),
        name="t5_bias_table",
    )(rel_bias, bucket)


def _layer_kernel(x_ref, mod_ref, sinks_ref, bias_ref, bsum_ref, bhalo_ref, ones_ref, w_in_ref, w_ap_ref,
                  w_pp_ref, w_out_ref, w_mix_ref, pscale_ref, lng_ref, lnb_ref, o_ref,
                  u_ref, q_ref, kp_ref, vp_ref, p_ref, sg_ref, ain_ref, halo_ref, h_ref, *, tm):
    i = pl.program_id(1)
    nblk = tm // BLOCK

    @pl.when(i == 0)
    def _():
        kp_ref[:, 0:BLOCK, :] = jnp.zeros((4, BLOCK, LANES), BF16)
        vp_ref[:, 0:BLOCK, :] = jnp.zeros((4, BLOCK, LANES), BF16)
        halo_ref[...] = jnp.zeros((POOL_HALO, D_MODEL), BF16)

    x = x_ref[0]
    shift = mod_ref[0, 0:1, :]
    scale = mod_ref[0, 1:2, :]
    gate = mod_ref[0, 2:3, :]
    u_ref[...] = (x * (1.0 + scale) + shift).astype(BF16)

    def proj(lo, hi):
        return jnp.dot(u_ref[...], w_in_ref[0, :, lo:hi], preferred_element_type=F32)

    qkv = proj(OFF_Q, OFF_AGATE)
    q_ref[...] = (qkv[:, 0:OFF_KV] * (HEAD_DIM ** -0.5 * LOG2E)).astype(BF16)
    lane = lax.broadcasted_iota(jnp.int32, (tm, LANES), 1)
    first_half = lane < HEAD_DIM
    for src, dst in ((qkv[:, OFF_KV:OFF_KV + LANES], kp_ref),
                     (qkv[:, OFF_KV + LANES:OFF_KV + 2 * LANES], vp_ref)):
        g0_lo = jnp.where(first_half, src, 0.0)
        g1_hi = jnp.where(first_half, 0.0, src)
        dst[0, BLOCK:, :] = g0_lo.astype(BF16)
        dst[1, BLOCK:, :] = pltpu.roll(g0_lo, HEAD_DIM, 1).astype(BF16)
        dst[2, BLOCK:, :] = pltpu.roll(g1_hi, HEAD_DIM, 1).astype(BF16)
        dst[3, BLOCK:, :] = g1_hi.astype(BF16)

    sg_ref[...] = _silu(proj(OFF_AGATE, OFF_PIN))

    qi = lax.broadcasted_iota(jnp.int32, (BLOCK, 2 * BLOCK), 0)
    ki = lax.broadcasted_iota(jnp.int32, (BLOCK, 2 * BLOCK), 1)
    dist = qi + BLOCK - ki
    band = (dist >= 0) & (dist < WINDOW)
    band_first = band & (ki >= jnp.where(i == 0, BLOCK, 0))
    lane_b = lax.broadcasted_iota(jnp.int32, (BLOCK, LANES), 1)
    first_half_b = lane_b < HEAD_DIM
    cw = (IN_WIDTH - OFF_PIN) // (nblk * N_KV_HEADS)

    for qb in range(nblk):
        r0 = qb * BLOCK
        mask = band_first if qb == 0 else band
        for g in range(N_KV_HEADS):
            c0 = g * LANE_TILES_PER_GROUP * LANES
            qs = jnp.concatenate(
                [q_ref[r0:r0 + BLOCK, c0 + j * LANES:c0 + (j + 1) * LANES]
                 for j in range(LANE_TILES_PER_GROUP)], axis=0)
            kcat = jnp.concatenate([kp_ref[2 * g, r0:r0 + 2 * BLOCK, :],
                                    kp_ref[2 * g + 1, r0:r0 + 2 * BLOCK, :]], axis=0)
            s = lax.dot_general(qs, kcat, (((1,), (1,)), ((), ())), preferred_element_type=F32)
            it = qb * N_KV_HEADS + g
            h_ref[:, it * cw:(it + 1) * cw] = proj(OFF_PIN + it * cw, OFF_PIN + (it + 1) * cw)
            sink_p = {}
            for j in range(LANE_TILES_PER_GROUP):
                for par in range(HEADS_PER_LANE_TILE):
                    h = g * Q_GROUP + j * HEADS_PER_LANE_TILE + par
                    sh = s[j * BLOCK:(j + 1) * BLOCK, par * 2 * BLOCK:(par + 1) * 2 * BLOCK]
                    logits = jnp.where(mask, sh + bias_ref[h], NEG_INF)
                    sink = sinks_ref[h] * LOG2E
                    m = jnp.maximum(jnp.max(logits, axis=-1, keepdims=True), sink)
                    p_ref[j * BLOCK:(j + 1) * BLOCK,
                          par * 2 * BLOCK:(par + 1) * 2 * BLOCK] = jnp.exp2(logits - m).astype(BF16)
                    sink_p[(j, par)] = jnp.exp2(sink - m)
            vcat = jnp.concatenate(
                [jnp.concatenate([vp_ref[2 * g, r0:r0 + 2 * BLOCK, :],
                                  vp_ref[2 * g + 1, r0:r0 + 2 * BLOCK, :]], axis=0),
                 ones_ref[...]], axis=1)
            o = jnp.dot(p_ref[...], vcat, preferred_element_type=F32)
            for j in range(LANE_TILES_PER_GROUP):
                cj = c0 + j * LANES
                oj = o[j * BLOCK:(j + 1) * BLOCK, :]
                denom = oj[:, LANES:] + jnp.where(first_half_b, sink_p[(j, 0)], sink_p[(j, 1)])
                ain_ref[r0:r0 + BLOCK, cj:cj + LANES] = (
                    oj[:, 0:LANES] / denom * sg_ref[r0:r0 + BLOCK, cj:cj + LANES]).astype(BF16)

    kp_ref[:, 0:BLOCK, :] = kp_ref[:, tm:tm + BLOCK, :]
    vp_ref[:, 0:BLOCK, :] = vp_ref[:, tm:tm + BLOCK, :]

    halo = halo_ref[...]
    for c in range(tm // POOL_ROWS):
        rs = slice(c * POOL_ROWS, (c + 1) * POOL_ROWS)
        a_proj = jnp.dot(ain_ref[rs, :], w_ap_ref[0], preferred_element_type=F32)

        pin = h_ref[rs, 0:D_MODEL]
        pin_b = pin.astype(BF16)
        sp = _silu(h_ref[rs, D_MODEL:2 * D_MODEL]) * pscale_ref[0]
        tpos = (i * tm + c * POOL_ROWS + 1
                + lax.broadcasted_iota(jnp.int32, (POOL_ROWS, 1), 0)).astype(F32)
        mixed = []
        for gi, w in enumerate(POOL_WINDOWS):
            cs = slice(gi * POOL_GROUP_WIDTH, (gi + 1) * POOL_GROUP_WIDTH)
            wsum = jnp.dot(bsum_ref[gi], pin_b[:, cs], preferred_element_type=F32)
            head = wsum[0:POOL_HALO] + jnp.dot(bhalo_ref[gi], halo[:, cs],
                                               preferred_element_type=F32)
            wsum = jnp.concatenate([head, wsum[POOL_HALO:]], axis=0)
            pooled = (wsum / jnp.minimum(tpos, float(w)) - pin[:, cs]).astype(BF16)
            mixed.append(jnp.dot(pooled, w_mix_ref[0, gi], preferred_element_type=F32))
        halo = pin_b[POOL_ROWS - POOL_HALO:, :]
        pp = (jnp.concatenate(mixed, axis=1) * sp).astype(BF16)
        p_proj = jnp.dot(pp, w_pp_ref[0], preferred_element_type=F32)

        y_in = (_sigmoid(h_ref[rs, 2 * D_MODEL:3 * D_MODEL]) * a_proj
                + _sigmoid(h_ref[rs, 3 * D_MODEL:4 * D_MODEL]) * p_proj).astype(BF16)
        y = jnp.dot(y_in, w_out_ref[0], preferred_element_type=F32) * (1.0 + gate)
        r = DEEPNORM_ALPHA * x_ref[0, rs, :] + y
        mu = jnp.mean(r, axis=-1, keepdims=True)
        rc = r - mu
        var = jnp.mean(rc * rc, axis=-1, keepdims=True)
        o_ref[0, rs, :] = rc * lax.rsqrt(var + LN_EPS) * lng_ref[0] + lnb_ref[0]
    halo_ref[...] = halo


def _pool_band_matrices():
    tm = POOL_ROWS
    t = np.arange(tm)[:, None]
    own = np.stack([((t - np.arange(tm)[None, :] >= 0) & (t - np.arange(tm)[None, :] < w))
                    for w in POOL_WINDOWS])
    th = np.arange(POOL_HALO)[:, None]
    back = th - (np.arange(POOL_HALO)[None, :] - POOL_HALO)
    halo = np.stack([back < w for w in POOL_WINDOWS])
    return own.astype(np.float32), halo.astype(np.float32)


def _layer(layer, x, mod, sinks, bias, w_in, w_ap, w_pp, w_out, w_mix, pscale, lng, lnb, *, tm):
    batch, seq, _ = x.shape
    n_groups = len(POOL_WINDOWS)
    bsum, bhalo = _pool_band_matrices()
    rows = np.arange(HEADS_PER_LANE_TILE * 2 * BLOCK)[:, None] // (2 * BLOCK)
    ones_pat = (rows == np.arange(LANES)[None, :] // HEAD_DIM).astype(np.float32)

    def const_spec(shape):
        zeros = (0,) * len(shape)
        return pl.BlockSpec(shape, lambda b, i: zeros, pipeline_mode=pl.Buffered(1))

    def layer_spec(shape):
        idx = (layer,) + (0,) * len(shape)
        return pl.BlockSpec((1,) + shape, lambda b, i: idx, pipeline_mode=pl.Buffered(1))

    return pl.pallas_call(
        functools.partial(_layer_kernel, tm=tm),
        grid=(batch, seq // tm),
        in_specs=[
            pl.BlockSpec((1, tm, D_MODEL), lambda b, i: (b, i, 0)),
            pl.BlockSpec((1, 3, D_MODEL), lambda b, i: (b, 0, 0)),
            pl.BlockSpec(memory_space=pltpu.SMEM),
            const_spec((N_Q_HEADS, BLOCK, 2 * BLOCK)),
            const_spec((n_groups, POOL_ROWS, POOL_ROWS)),
            const_spec((n_groups, POOL_HALO, POOL_HALO)),
            const_spec(ones_pat.shape),
            layer_spec((D_MODEL, IN_WIDTH)),
            layer_spec((D_MODEL, D_MODEL)),
            layer_spec((D_MODEL, D_MODEL)),
            layer_spec((D_MODEL, D_MODEL)),
            layer_spec((n_groups, POOL_GROUP_WIDTH, POOL_GROUP_WIDTH)),
            layer_spec((1, D_MODEL)),
            layer_spec((1, D_MODEL)),
            layer_spec((1, D_MODEL)),
        ],
        out_specs=pl.BlockSpec((1, tm, D_MODEL), lambda b, i: (b, i, 0)),
        out_shape=jax.ShapeDtypeStruct(x.shape, F32),
        scratch_shapes=[
            pltpu.VMEM((tm, D_MODEL), BF16),
            pltpu.VMEM((tm, D_MODEL), BF16),
            pltpu.VMEM((4, tm + BLOCK, LANES), BF16),
            pltpu.VMEM((4, tm + BLOCK, LANES), BF16),
            pltpu.VMEM((LANE_TILES_PER_GROUP * BLOCK, HEADS_PER_LANE_TILE * 2 * BLOCK), BF16),
            pltpu.VMEM((tm, D_MODEL), F32),
            pltpu.VMEM((tm, D_MODEL), BF16),
            pltpu.VMEM((POOL_HALO, D_MODEL), BF16),
            pltpu.VMEM((tm, IN_WIDTH - OFF_PIN), F32),
        ],
        compiler_params=pltpu.CompilerParams(
            dimension_semantics=("arbitrary", "arbitrary"),
            vmem_limit_bytes=VMEM_LIMIT_BYTES),
        name="fused_layer",
    )(x, mod, sinks, bias, jnp.asarray(bsum, BF16), jnp.asarray(bhalo, BF16),
      jnp.asarray(ones_pat, BF16), w_in, w_ap, w_pp, w_out, w_mix, pscale, lng, lnb)


@functools.partial(jax.jit, static_argnames=("tm",))
def _forward(x, c, rel_bias, w_ada, b_ada, w_in, sinks, w_pool_mix, pool_scale,
             w_attn_proj, w_pool_proj, w_out, ln_gain, ln_bias, tm=512):
    batch = x.shape[0]
    mod = _adaln(c, w_ada, b_ada).reshape(DEPTH, batch, 3, D_MODEL)
    bias = _bias_table(rel_bias)
    w_in, w_ap, w_pp, w_o, w_mix = (w.astype(BF16) for w in
                                    (w_in, w_attn_proj, w_pool_proj, w_out, w_pool_mix))
    pscale, lng, lnb = (v.reshape(DEPTH, 1, D_MODEL) for v in (pool_scale, ln_gain, ln_bias))
    for l in range(DEPTH):
        x = _layer(l, x, mod[l], sinks[l], bias, w_in, w_ap, w_pp, w_o, w_mix,
                   pscale, lng, lnb, tm=tm)
    return x


def kernel(x, c, rel_bias, w_ada, b_ada, w_in, sinks, w_pool_mix, pool_scale, w_attn_proj,
           w_pool_proj, w_out, ln_gain, ln_bias):
    return _forward(x, c, rel_bias, w_ada, b_ada, w_in, sinks, w_pool_mix, pool_scale,
                    w_attn_proj, w_pool_proj, w_out, ln_gain, ln_bias)
```

```python
import functools
import math

import numpy as np
import jax
import jax.numpy as jnp
from jax import lax
from jax.experimental import pallas as pl
from jax.experimental.pallas import tpu as pltpu

D_MODEL = 1024
DEPTH = 2
N_Q_HEADS = 16
N_KV_HEADS = 2
HEAD_DIM = 64
Q_GROUP = N_Q_HEADS // N_KV_HEADS
WINDOW = 128
BLOCK = 128
NEG_INF = -1e30
NUM_BUCKETS = 32
MAX_DISTANCE = 128
POOL_WINDOWS = (2, 4, 8, 16)
POOL_GROUP_WIDTH = D_MODEL // len(POOL_WINDOWS)
POOL_HALO = 16
DEEPNORM_ALPHA = (2 * DEPTH) ** 0.25
LN_EPS = 1e-5
LOG2E = math.log2(math.e)
POOL_ROWS = 256

OFF_Q = 0
OFF_KV = 1024
OFF_AGATE = 1280
OFF_PIN = 2304
OFF_PGATE = 3328
OFF_MERGE = 4352
IN_WIDTH = 6400

LANES = 128
HEADS_PER_LANE_TILE = LANES // HEAD_DIM
LANE_TILES_PER_GROUP = Q_GROUP // HEADS_PER_LANE_TILE
VMEM_LIMIT_BYTES = 56 * 1024 * 1024

F32 = jnp.float32
BF16 = jnp.bfloat16


def _bucket_map():
    q = np.arange(BLOCK)[:, None]
    k = np.arange(2 * BLOCK)[None, :]
    dist = np.maximum(q + BLOCK - k, 0)
    max_exact = NUM_BUCKETS // 2
    d = np.maximum(dist, 1).astype(np.float64)
    large = max_exact + (np.log(d / max_exact) / math.log(MAX_DISTANCE / max_exact)
                         * (NUM_BUCKETS - max_exact)).astype(np.int32)
    large = np.minimum(large, NUM_BUCKETS - 1)
    return np.where(dist < max_exact, dist, large).astype(np.int32)


def _sigmoid(x):
    return 1.0 / (1.0 + jnp.exp2(x * (-LOG2E)))


def _silu(x):
    return x * _sigmoid(x)


def _adaln_kernel(c_ref, w_ref, b_ref, o_ref):
    sc = _silu(c_ref[...])
    o_ref[0] = jnp.dot(sc, w_ref[0], preferred_element_type=F32,
                       precision=lax.Precision.HIGHEST) + b_ref[0]


def _adaln(c, w_ada, b_ada):
    batch = c.shape[0]
    tn = 1024
    n_out = w_ada.shape[-1]
    return pl.pallas_call(
        _adaln_kernel,
        grid=(DEPTH, n_out // tn),
        in_specs=[
            pl.BlockSpec((batch, D_MODEL), lambda l, n: (0, 0)),
            pl.BlockSpec((1, D_MODEL, tn), lambda l, n: (l, 0, n)),
            pl.BlockSpec((1, 1, tn), lambda l, n: (l, 0, n)),
        ],
        out_specs=pl.BlockSpec((1, batch, tn), lambda l, n: (l, 0, n)),
        out_shape=jax.ShapeDtypeStruct((DEPTH, batch, n_out), F32),
        name="adaln_mod",
    )(c, w_ada, b_ada.reshape(DEPTH, 1, n_out))


BIAS_HEADS_PER_STEP = 8
BIAS_ROW_CHUNK = 32


def _bias_kernel(rel_ref, bucket_ref, o_ref):
    h0 = pl.program_id(0) * BIAS_HEADS_PER_STEP
    for r0 in range(0, BLOCK, BIAS_ROW_CHUNK):
        bucket = bucket_ref[r0:r0 + BIAS_ROW_CHUNK, :]
        acc = [jnp.zeros(bucket.shape, F32) for _ in range(BIAS_HEADS_PER_STEP)]
        for b in range(NUM_BUCKETS):
            hit = bucket == b
            for hh in range(BIAS_HEADS_PER_STEP):
                acc[hh] = jnp.where(hit, rel_ref[b, h0 + hh] * LOG2E, acc[hh])
        for hh in range(BIAS_HEADS_PER_STEP):
            o_ref[hh, r0:r0 + BIAS_ROW_CHUNK, :] = acc[hh]


def _bias_table(rel_bias):
    bucket = jnp.asarray(_bucket_map())
    return pl.pallas_call(
        _bias_kernel,
        grid=(N_Q_HEADS // BIAS_HEADS_PER_STEP,),
        in_specs=[
            pl.BlockSpec(memory_space=pltpu.SMEM),
            pl.BlockSpec((BLOCK, 2 * BLOCK), lambda h: (0, 0)),
        ],
        out_specs=pl.BlockSpec((BIAS_HEADS_PER_STEP, BLOCK, 2 * BLOCK), lambda h: (h, 0, 0)),
        out_shape=jax.ShapeDtypeStruct((N_Q_HEADS, BLOCK, 2 * BLOCK), F32),
        name="t5_bias_table",
    )(rel_bias, bucket)


def _layer_kernel(x_ref, mod_ref, sinks_ref, bias_ref, bsum_ref, bhalo_ref, ones_ref, w_in_ref, w_ap_ref,
                  w_pp_ref, w_out_ref, w_mix_ref, pscale_ref, lng_ref, lnb_ref, o_ref,
                  u_ref, q_ref, kp_ref, vp_ref, p_ref, sg_ref, ain_ref, halo_ref, h_ref,
                  yin_ref, xprev_ref, *, tm):
    i = pl.program_id(1)
    n_tiles = pl.num_programs(1) - 1
    nblk = tm // BLOCK
    n_chunks = tm // POOL_ROWS
    n_iter = nblk * N_KV_HEADS
    cw = (IN_WIDTH - OFF_PIN) // n_iter

    @pl.when(i == 0)
    def _():
        kp_ref[:, 0:BLOCK, :] = jnp.zeros((4, BLOCK, LANES), BF16)
        vp_ref[:, 0:BLOCK, :] = jnp.zeros((4, BLOCK, LANES), BF16)
        halo_ref[...] = jnp.zeros((POOL_HALO, D_MODEL), BF16)
        yin_ref[...] = jnp.zeros((tm, D_MODEL), BF16)
        xprev_ref[...] = jnp.zeros((tm, D_MODEL), F32)

    def proj(lo, hi):
        return jnp.dot(u_ref[...], w_in_ref[0, :, lo:hi], preferred_element_type=F32)

    def finish_previous_tile():
        gate = mod_ref[0, 2:3, :]
        for c in range(n_chunks):
            rs = slice(c * POOL_ROWS, (c + 1) * POOL_ROWS)
            y = jnp.dot(yin_ref[rs, :], w_out_ref[0], preferred_element_type=F32) * (1.0 + gate)
            r = DEEPNORM_ALPHA * xprev_ref[rs, :] + y
            mu = jnp.mean(r, axis=-1, keepdims=True)
            rc = r - mu
            var = jnp.mean(rc * rc, axis=-1, keepdims=True)
            o_ref[0, rs, :] = rc * lax.rsqrt(var + LN_EPS) * lng_ref[0] + lnb_ref[0]

    def tile_head():
        x = x_ref[0]
        shift = mod_ref[0, 0:1, :]
        scale = mod_ref[0, 1:2, :]
        u_ref[...] = (x * (1.0 + scale) + shift).astype(BF16)
        xprev_ref[...] = x

        qkv = proj(OFF_Q, OFF_AGATE)
        q_ref[...] = (qkv[:, 0:OFF_KV] * (HEAD_DIM ** -0.5 * LOG2E)).astype(BF16)
        lane = lax.broadcasted_iota(jnp.int32, (tm, LANES), 1)
        first_half = lane < HEAD_DIM
        for src, dst in ((qkv[:, OFF_KV:OFF_KV + LANES], kp_ref),
                         (qkv[:, OFF_KV + LANES:OFF_KV + 2 * LANES], vp_ref)):
            g0_lo = jnp.where(first_half, src, 0.0)
            g1_hi = jnp.where(first_half, 0.0, src)
            dst[0, BLOCK:, :] = g0_lo.astype(BF16)
            dst[1, BLOCK:, :] = pltpu.roll(g0_lo, HEAD_DIM, 1).astype(BF16)
            dst[2, BLOCK:, :] = pltpu.roll(g1_hi, HEAD_DIM, 1).astype(BF16)
            dst[3, BLOCK:, :] = g1_hi.astype(BF16)

        sg_ref[...] = _silu(proj(OFF_AGATE, OFF_PIN))

    def tile_rest():
        qi = lax.broadcasted_iota(jnp.int32, (BLOCK, 2 * BLOCK), 0)
        ki = lax.broadcasted_iota(jnp.int32, (BLOCK, 2 * BLOCK), 1)
        dist = qi + BLOCK - ki
        band = (dist >= 0) & (dist < WINDOW)
        band_first = band & (ki >= jnp.where(i == 0, BLOCK, 0))
        lane_b = lax.broadcasted_iota(jnp.int32, (BLOCK, LANES), 1)
        first_half_b = lane_b < HEAD_DIM

        def deferred_projection(it):
            hv = proj(OFF_PIN + it * cw, OFF_PIN + (it + 1) * cw)
            segment, seg_off = divmod(it * cw, D_MODEL)
            if segment == 1:
                hv = _silu(hv) * pscale_ref[0, :, seg_off:seg_off + cw]
            elif segment >= 2:
                hv = _sigmoid(hv)
            h_ref[:, it * cw:(it + 1) * cw] = hv

        def qk_logits(qb, g):
            r0 = qb * BLOCK
            c0 = g * LANE_TILES_PER_GROUP * LANES
            qs = jnp.concatenate(
                [q_ref[r0:r0 + BLOCK, c0 + j * LANES:c0 + (j + 1) * LANES]
                 for j in range(LANE_TILES_PER_GROUP)], axis=0)
            kcat = jnp.concatenate([kp_ref[2 * g, r0:r0 + 2 * BLOCK, :],
                                    kp_ref[2 * g + 1, r0:r0 + 2 * BLOCK, :]], axis=0)
            return lax.dot_general(qs, kcat, (((1,), (1,)), ((), ())), preferred_element_type=F32)

        def softmax_numerators(qb, g, s):
            mask = band_first if qb == 0 else band
            sink_p = {}
            for j in range(LANE_TILES_PER_GROUP):
                for par in range(HEADS_PER_LANE_TILE):
                    h = g * Q_GROUP + j * HEADS_PER_LANE_TILE + par
                    sh = s[j * BLOCK:(j + 1) * BLOCK, par * 2 * BLOCK:(par + 1) * 2 * BLOCK]
                    logits = jnp.where(mask, sh + bias_ref[h], NEG_INF)
                    sink = sinks_ref[h] * LOG2E
                    m = jnp.maximum(jnp.max(logits, axis=-1, keepdims=True), sink)
                    p_ref[j * BLOCK:(j + 1) * BLOCK,
                          par * 2 * BLOCK:(par + 1) * 2 * BLOCK] = jnp.exp2(logits - m).astype(BF16)
                    sink_p[(j, par)] = jnp.exp2(sink - m)
            return sink_p

        def attend(qb, g, sink_p):
            r0 = qb * BLOCK
            c0 = g * LANE_TILES_PER_GROUP * LANES
            vcat = jnp.concatenate(
                [jnp.concatenate([vp_ref[2 * g, r0:r0 + 2 * BLOCK, :],
                                  vp_ref[2 * g + 1, r0:r0 + 2 * BLOCK, :]], axis=0),
                 ones_ref[...]], axis=1)
            o = jnp.dot(p_ref[...], vcat, preferred_element_type=F32)
            for j in range(LANE_TILES_PER_GROUP):
                cj = c0 + j * LANES
                oj = o[j * BLOCK:(j + 1) * BLOCK, :]
                denom = oj[:, LANES:] + jnp.where(first_half_b, sink_p[(j, 0)], sink_p[(j, 1)])
                ain_ref[r0:r0 + BLOCK, cj:cj + LANES] = (
                    oj[:, 0:LANES] / denom * sg_ref[r0:r0 + BLOCK, cj:cj + LANES]).astype(BF16)

        pool = {"halo": halo_ref[...]}

        def pool_window_sums(c):
            rs = slice(c * POOL_ROWS, (c + 1) * POOL_ROWS)
            pin = h_ref[rs, 0:D_MODEL]
            pin_b = pin.astype(BF16)
            tpos = (i * tm + c * POOL_ROWS + 1
                    + lax.broadcasted_iota(jnp.int32, (POOL_ROWS, 1), 0)).astype(F32)
            pooled = []
            for gi, w in enumerate(POOL_WINDOWS):
                cs = slice(gi * POOL_GROUP_WIDTH, (gi + 1) * POOL_GROUP_WIDTH)
                wsum = jnp.dot(bsum_ref[gi], pin_b[:, cs], preferred_element_type=F32)
                head = wsum[0:POOL_HALO] + jnp.dot(bhalo_ref[gi], pool["halo"][:, cs],
                                                   preferred_element_type=F32)
                wsum = jnp.concatenate([head, wsum[POOL_HALO:]], axis=0)
                pooled.append((wsum / jnp.minimum(tpos, float(w)) - pin[:, cs]).astype(BF16))
            pool["halo"] = pin_b[POOL_ROWS - POOL_HALO:, :]
            pool[("pooled", c)] = pooled

        def pool_mix(c):
            rs = slice(c * POOL_ROWS, (c + 1) * POOL_ROWS)
            pooled = pool.pop(("pooled", c))
            mixed = [jnp.dot(pooled[gi], w_mix_ref[0, gi], preferred_element_type=F32)
                     for gi in range(len(POOL_WINDOWS))]
            pool[("pp", c)] = (jnp.concatenate(mixed, axis=1)
                               * h_ref[rs, D_MODEL:2 * D_MODEL]).astype(BF16)

        def pool_project(c):
            pool[("p_proj", c)] = jnp.dot(pool.pop(("pp", c)), w_pp_ref[0], preferred_element_type=F32)

        def attn_project(c):
            rs = slice(c * POOL_ROWS, (c + 1) * POOL_ROWS)
            pool[("a_proj", c)] = jnp.dot(ain_ref[rs, :], w_ap_ref[0], preferred_element_type=F32)

        def merge(c):
            rs = slice(c * POOL_ROWS, (c + 1) * POOL_ROWS)
            yin_ref[rs, :] = (h_ref[rs, 2 * D_MODEL:3 * D_MODEL] * pool.pop(("a_proj", c))
                              + h_ref[rs, 3 * D_MODEL:4 * D_MODEL] * pool.pop(("p_proj", c))
                              ).astype(BF16)

        extra = {it: [] for it in range(n_iter)}
        pin_done = n_iter // 4
        gate_done = n_iter // 2
        for c in range(n_chunks):
            extra[pin_done].append(functools.partial(pool_window_sums, c))
            extra[pin_done + 1].append(functools.partial(pool_mix, c))
            extra[min(gate_done + c, n_iter - 1)].append(functools.partial(pool_project, c))
        for c in range(n_chunks - 1):
            extra[min(4 * c + 7, n_iter - 1)].append(functools.partial(attn_project, c))

        for qb in range(nblk):
            for g in range(N_KV_HEADS):
                it = qb * N_KV_HEADS + g
                last = it == n_iter - 1
                s = qk_logits(qb, g)
                if not last:
                    deferred_projection(it)
                for fn in extra[it]:
                    fn()
                sink_p = softmax_numerators(qb, g, s)
                attend(qb, g, sink_p)
                if last:
                    deferred_projection(it)
        attn_project(n_chunks - 1)

        kp_ref[:, 0:BLOCK, :] = kp_ref[:, tm:tm + BLOCK, :]
        vp_ref[:, 0:BLOCK, :] = vp_ref[:, tm:tm + BLOCK, :]
        halo_ref[...] = pool["halo"]
        for c in range(n_chunks):
            merge(c)

    @pl.when(i < n_tiles)
    def _():
        finish_previous_tile()
        tile_head()

    @pl.when(jnp.logical_and(i >= 0, i < n_tiles))
    def _():
        tile_rest()

    @pl.when(i == n_tiles)
    def _():
        finish_previous_tile()


def _pool_band_matrices():
    tm = POOL_ROWS
    t = np.arange(tm)[:, None]
    own = np.stack([((t - np.arange(tm)[None, :] >= 0) & (t - np.arange(tm)[None, :] < w))
                    for w in POOL_WINDOWS])
    th = np.arange(POOL_HALO)[:, None]
    back = th - (np.arange(POOL_HALO)[None, :] - POOL_HALO)
    halo = np.stack([back < w for w in POOL_WINDOWS])
    return own.astype(np.float32), halo.astype(np.float32)


def _layer(layer, x, mod, sinks, bias, w_in, w_ap, w_pp, w_out, w_mix, pscale, lng, lnb, *, tm):
    batch, seq, _ = x.shape
    n_tiles = seq // tm
    n_groups = len(POOL_WINDOWS)
    bsum, bhalo = _pool_band_matrices()
    rows = np.arange(HEADS_PER_LANE_TILE * 2 * BLOCK)[:, None] // (2 * BLOCK)
    ones_pat = (rows == np.arange(LANES)[None, :] // HEAD_DIM).astype(np.float32)

    def const_spec(shape):
        zeros = (0,) * len(shape)
        return pl.BlockSpec(shape, lambda b, i: zeros, pipeline_mode=pl.Buffered(1))

    def layer_spec(shape):
        idx = (layer,) + (0,) * len(shape)
        return pl.BlockSpec((1,) + shape, lambda b, i: idx, pipeline_mode=pl.Buffered(1))

    return pl.pallas_call(
        functools.partial(_layer_kernel, tm=tm),
        grid=(batch, n_tiles + 1),
        in_specs=[
            pl.BlockSpec((1, tm, D_MODEL), lambda b, i: (b, jnp.minimum(i, n_tiles - 1), 0)),
            pl.BlockSpec((1, 3, D_MODEL), lambda b, i: (b, 0, 0)),
            pl.BlockSpec(memory_space=pltpu.SMEM),
            const_spec((N_Q_HEADS, BLOCK, 2 * BLOCK)),
            const_spec((n_groups, POOL_ROWS, POOL_ROWS)),
            const_spec((n_groups, POOL_HALO, POOL_HALO)),
            const_spec(ones_pat.shape),
            layer_spec((D_MODEL, IN_WIDTH)),
            layer_spec((D_MODEL, D_MODEL)),
            layer_spec((D_MODEL, D_MODEL)),
            layer_spec((D_MODEL, D_MODEL)),
            layer_spec((n_groups, POOL_GROUP_WIDTH, POOL_GROUP_WIDTH)),
            layer_spec((1, D_MODEL)),
            layer_spec((1, D_MODEL)),
            layer_spec((1, D_MODEL)),
        ],
        out_specs=pl.BlockSpec((1, tm, D_MODEL), lambda b, i: (b, jnp.maximum(i - 1, 0), 0)),
        out_shape=jax.ShapeDtypeStruct(x.shape, F32),
        scratch_shapes=[
            pltpu.VMEM((tm, D_MODEL), BF16),
            pltpu.VMEM((tm, D_MODEL), BF16),
            pltpu.VMEM((4, tm + BLOCK, LANES), BF16),
            pltpu.VMEM((4, tm + BLOCK, LANES), BF16),
            pltpu.VMEM((LANE_TILES_PER_GROUP * BLOCK, HEADS_PER_LANE_TILE * 2 * BLOCK), BF16),
            pltpu.VMEM((tm, D_MODEL), F32),
            pltpu.VMEM((tm, D_MODEL), BF16),
            pltpu.VMEM((POOL_HALO, D_MODEL), BF16),
            pltpu.VMEM((tm, IN_WIDTH - OFF_PIN), F32),
            pltpu.VMEM((tm, D_MODEL), BF16),
            pltpu.VMEM((tm, D_MODEL), F32),
        ],
        compiler_params=pltpu.CompilerParams(
            dimension_semantics=("arbitrary", "arbitrary"),
            vmem_limit_bytes=VMEM_LIMIT_BYTES),
        name="fused_layer",
    )(x, mod, sinks, bias, jnp.asarray(bsum, BF16), jnp.asarray(bhalo, BF16),
      jnp.asarray(ones_pat, BF16), w_in, w_ap, w_pp, w_out, w_mix, pscale, lng, lnb)


@functools.partial(jax.jit, static_argnames=("tm",))
def _forward(x, c, rel_bias, w_ada, b_ada, w_in, sinks, w_pool_mix, pool_scale,
             w_attn_proj, w_pool_proj, w_out, ln_gain, ln_bias, tm=512):
    batch = x.shape[0]
    mod = _adaln(c, w_ada, b_ada).reshape(DEPTH, batch, 3, D_MODEL)
    bias = _bias_table(rel_bias)
    w_in, w_ap, w_pp, w_o, w_mix = (w.astype(BF16) for w in
                                    (w_in, w_attn_proj, w_pool_proj, w_out, w_pool_mix))
    pscale, lng, lnb = (v.reshape(DEPTH, 1, D_MODEL) for v in (pool_scale, ln_gain, ln_bias))
    for l in range(DEPTH):
        x = _layer(l, x, mod[l], sinks[l], bias, w_in, w_ap, w_pp, w_o, w_mix,
                   pscale, lng, lnb, tm=tm)
    return x


def kernel(x, c, rel_bias, w_ada, b_ada, w_in, sinks, w_pool_mix, pool_scale, w_attn_proj,
           w_pool_proj, w_out, ln_gain, ln_bias):
    return _forward(x, c, rel_bias, w_ada, b_ada, w_in, sinks, w_pool_mix, pool_scale,
                    w_attn_proj, w_pool_proj, w_out, ln_gain, ln_bias)
```

```python
import functools
import math

import numpy as np
import jax
import jax.numpy as jnp
from jax import lax
from jax.experimental import pallas as pl
from jax.experimental.pallas import tpu as pltpu

D_MODEL = 1024
DEPTH = 2
N_Q_HEADS = 16
N_KV_HEADS = 2
HEAD_DIM = 64
Q_GROUP = N_Q_HEADS // N_KV_HEADS
WINDOW = 128
BLOCK = 128
NEG_INF = -1e30
NUM_BUCKETS = 32
MAX_DISTANCE = 128
POOL_WINDOWS = (2, 4, 8, 16)
POOL_GROUP_WIDTH = D_MODEL // len(POOL_WINDOWS)
POOL_HALO = 16
DEEPNORM_ALPHA = (2 * DEPTH) ** 0.25
LN_EPS = 1e-5
LOG2E = math.log2(math.e)
POOL_ROWS = 256
STAGE_IN_ROWS = 64
STAGE_SQ_ROWS = 256

OFF_Q = 0
OFF_KV = 1024
OFF_AGATE = 1280
OFF_PIN = 2304
OFF_PGATE = 3328
OFF_MERGE = 4352
IN_WIDTH = 6400

LANES = 128
HEADS_PER_LANE_TILE = LANES // HEAD_DIM
LANE_TILES_PER_GROUP = Q_GROUP // HEADS_PER_LANE_TILE
VMEM_LIMIT_BYTES = 56 * 1024 * 1024

F32 = jnp.float32
BF16 = jnp.bfloat16


def _bucket_map():
    q = np.arange(BLOCK)[:, None]
    k = np.arange(2 * BLOCK)[None, :]
    dist = np.maximum(q + BLOCK - k, 0)
    max_exact = NUM_BUCKETS // 2
    d = np.maximum(dist, 1).astype(np.float64)
    large = max_exact + (np.log(d / max_exact) / math.log(MAX_DISTANCE / max_exact)
                         * (NUM_BUCKETS - max_exact)).astype(np.int32)
    large = np.minimum(large, NUM_BUCKETS - 1)
    return np.where(dist < max_exact, dist, large).astype(np.int32)


def _sigmoid(x):
    return 1.0 / (1.0 + jnp.exp2(x * (-LOG2E)))


def _silu(x):
    return x * _sigmoid(x)


def _adaln_kernel(c_ref, w_ref, b_ref, o_ref):
    sc = _silu(c_ref[...])
    o_ref[0] = jnp.dot(sc, w_ref[0], preferred_element_type=F32) + b_ref[0]


def _adaln(c, w_ada, b_ada):
    batch = c.shape[0]
    tn = 1024
    n_out = w_ada.shape[-1]
    return pl.pallas_call(
        _adaln_kernel,
        grid=(DEPTH, n_out // tn),
        in_specs=[
            pl.BlockSpec((batch, D_MODEL), lambda l, n: (0, 0)),
            pl.BlockSpec((1, D_MODEL, tn), lambda l, n: (l, 0, n)),
            pl.BlockSpec((1, 1, tn), lambda l, n: (l, 0, n)),
        ],
        out_specs=pl.BlockSpec((1, batch, tn), lambda l, n: (l, 0, n)),
        out_shape=jax.ShapeDtypeStruct((DEPTH, batch, n_out), F32),
        name="adaln_mod",
    )(c, w_ada, b_ada.reshape(DEPTH, 1, n_out))


BIAS_HEADS_PER_STEP = 8
BIAS_ROW_CHUNK = 32


def _bias_kernel(rel_ref, bucket_ref, o_ref):
    h0 = pl.program_id(0) * BIAS_HEADS_PER_STEP
    for r0 in range(0, BLOCK, BIAS_ROW_CHUNK):
        bucket = bucket_ref[r0:r0 + BIAS_ROW_CHUNK, :]
        acc = [jnp.zeros(bucket.shape, F32) for _ in range(BIAS_HEADS_PER_STEP)]
        for b in range(NUM_BUCKETS):
            hit = bucket == b
            for hh in range(BIAS_HEADS_PER_STEP):
                acc[hh] = jnp.where(hit, rel_ref[b, h0 + hh] * LOG2E, acc[hh])
        for hh in range(BIAS_HEADS_PER_STEP):
            o_ref[hh, r0:r0 + BIAS_ROW_CHUNK, :] = acc[hh]


def _bias_table(rel_bias):
    bucket = jnp.asarray(_bucket_map())
    return pl.pallas_call(
        _bias_kernel,
        grid=(N_Q_HEADS // BIAS_HEADS_PER_STEP,),
        in_specs=[
            pl.BlockSpec(memory_space=pltpu.SMEM),
            pl.BlockSpec((BLOCK, 2 * BLOCK), lambda h: (0, 0)),
        ],
        out_specs=pl.BlockSpec((BIAS_HEADS_PER_STEP, BLOCK, 2 * BLOCK), lambda h: (h, 0, 0)),
        out_shape=jax.ShapeDtypeStruct((N_Q_HEADS, BLOCK, 2 * BLOCK), F32),
        name="t5_bias_table",
    )(rel_bias, bucket)


def _layer_kernel(x_ref, mod_ref, sinks_ref, bias_ref, bsum_ref, bhalo_ref, ones_ref,
                  w_in_hbm, w_ap_hbm, w_pp_hbm, w_out_hbm, w_mix_hbm,
                  pscale_ref, lng_ref, lnb_ref, o_ref,
                  w_in_ref, w_ap_ref, w_pp_ref, w_out_ref, w_mix_ref,
                  stage_in_ref, stage_sq_ref, stage_mix_ref, load_sem,
                  u_ref, q_ref, kp_ref, vp_ref, p_ref, sg_ref, ain_ref, halo_ref, h_ref,
                  yin_ref, xprev_ref, *, tm, layer):
    i = pl.program_id(1)
    n_tiles = pl.num_programs(1) - 1
    nblk = tm // BLOCK
    n_chunks = tm // POOL_ROWS
    n_iter = nblk * N_KV_HEADS
    cw = (IN_WIDTH - OFF_PIN) // n_iter

    @pl.when(jnp.logical_and(pl.program_id(0) == 0, i == 0))
    def _():
        copies = []

        def add(hbm_rows, stage, sem, dst):
            copies.append((pltpu.make_async_copy(hbm_rows, stage, sem), stage, dst))

        for k in range(D_MODEL // STAGE_IN_ROWS):
            rows = slice(k * STAGE_IN_ROWS, (k + 1) * STAGE_IN_ROWS)
            add(w_in_hbm.at[layer, rows, :], stage_in_ref.at[k % 2], load_sem.at[k % 2],
                w_in_ref.at[rows, :])
        for hbm, dst in ((w_ap_hbm, w_ap_ref), (w_pp_hbm, w_pp_ref), (w_out_hbm, w_out_ref)):
            for k in range(D_MODEL // STAGE_SQ_ROWS):
                rows = slice(k * STAGE_SQ_ROWS, (k + 1) * STAGE_SQ_ROWS)
                add(hbm.at[layer, rows, :], stage_sq_ref.at[k % 2], load_sem.at[2 + k % 2],
                    dst.at[rows, :])
        add(w_mix_hbm.at[layer], stage_mix_ref, load_sem.at[4], w_mix_ref)

        for k in range(min(2, len(copies))):
            copies[k][0].start()
        for k, (copy, stage, dst) in enumerate(copies):
            copy.wait()
            dst[...] = stage[...].astype(BF16)
            if k + 2 < len(copies):
                copies[k + 2][0].start()

    @pl.when(i == 0)
    def _():
        kp_ref[:, 0:BLOCK, :] = jnp.zeros((4, BLOCK, LANES), BF16)
        vp_ref[:, 0:BLOCK, :] = jnp.zeros((4, BLOCK, LANES), BF16)
        halo_ref[...] = jnp.zeros((POOL_HALO, D_MODEL), BF16)
        yin_ref[...] = jnp.zeros((tm, D_MODEL), BF16)
        xprev_ref[...] = jnp.zeros((tm, D_MODEL), F32)

    def proj(lo, hi):
        return jnp.dot(u_ref[...], w_in_ref[:, lo:hi], preferred_element_type=F32)

    def finish_previous_tile():
        gate = mod_ref[0, 2:3, :]
        for c in range(n_chunks):
            rs = slice(c * POOL_ROWS, (c + 1) * POOL_ROWS)
            y = jnp.dot(yin_ref[rs, :], w_out_ref[...], preferred_element_type=F32) * (1.0 + gate)
            r = DEEPNORM_ALPHA * xprev_ref[rs, :] + y
            mu = jnp.mean(r, axis=-1, keepdims=True)
            rc = r - mu
            var = jnp.mean(rc * rc, axis=-1, keepdims=True)
            o_ref[0, rs, :] = rc * lax.rsqrt(var + LN_EPS) * lng_ref[0] + lnb_ref[0]

    def tile_head():
        x = x_ref[0]
        shift = mod_ref[0, 0:1, :]
        scale1 = 1.0 + mod_ref[0, 1:2, :]
        u_ref[...] = (x * scale1 + shift).astype(BF16)
        xprev_ref[...] = x

        qkv = proj(OFF_Q, OFF_AGATE)
        q_ref[...] = (qkv[:, 0:OFF_KV] * (HEAD_DIM ** -0.5 * LOG2E)).astype(BF16)
        lane = lax.broadcasted_iota(jnp.int32, (tm, LANES), 1)
        first_half = lane < HEAD_DIM
        for src, dst in ((qkv[:, OFF_KV:OFF_KV + LANES], kp_ref),
                         (qkv[:, OFF_KV + LANES:OFF_KV + 2 * LANES], vp_ref)):
            g0_lo = jnp.where(first_half, src, 0.0)
            g1_hi = jnp.where(first_half, 0.0, src)
            dst[0, BLOCK:, :] = g0_lo.astype(BF16)
            dst[1, BLOCK:, :] = pltpu.roll(g0_lo, HEAD_DIM, 1).astype(BF16)
            dst[2, BLOCK:, :] = pltpu.roll(g1_hi, HEAD_DIM, 1).astype(BF16)
            dst[3, BLOCK:, :] = g1_hi.astype(BF16)

    def tile_rest():
        sg_ref[...] = _silu(proj(OFF_AGATE, OFF_PIN))

        qi = lax.broadcasted_iota(jnp.int32, (BLOCK, 2 * BLOCK), 0)
        ki = lax.broadcasted_iota(jnp.int32, (BLOCK, 2 * BLOCK), 1)
        dist = qi + BLOCK - ki
        band = (dist >= 0) & (dist < WINDOW)
        band_first = band & (ki >= jnp.where(i == 0, BLOCK, 0))
        lane_b = lax.broadcasted_iota(jnp.int32, (BLOCK, LANES), 1)
        first_half_b = lane_b < HEAD_DIM

        def deferred_projection(it):
            hv = proj(OFF_PIN + it * cw, OFF_PIN + (it + 1) * cw)
            segment, seg_off = divmod(it * cw, D_MODEL)
            if segment == 1:
                hv = _silu(hv) * pscale_ref[0, :, seg_off:seg_off + cw]
            elif segment >= 2:
                hv = _sigmoid(hv)
            h_ref[:, it * cw:(it + 1) * cw] = hv

        def qk_logits(qb, g):
            r0 = qb * BLOCK
            c0 = g * LANE_TILES_PER_GROUP * LANES
            qs = jnp.concatenate(
                [q_ref[r0:r0 + BLOCK, c0 + j * LANES:c0 + (j + 1) * LANES]
                 for j in range(LANE_TILES_PER_GROUP)], axis=0)
            kcat = jnp.concatenate([kp_ref[2 * g, r0:r0 + 2 * BLOCK, :],
                                    kp_ref[2 * g + 1, r0:r0 + 2 * BLOCK, :]], axis=0)
            return lax.dot_general(qs, kcat, (((1,), (1,)), ((), ())), preferred_element_type=F32)

        def softmax_numerators(qb, g, s):
            mask = band_first if qb == 0 else band
            sink_p = {}
            for j in range(LANE_TILES_PER_GROUP):
                for par in range(HEADS_PER_LANE_TILE):
                    h = g * Q_GROUP + j * HEADS_PER_LANE_TILE + par
                    sh = s[j * BLOCK:(j + 1) * BLOCK, par * 2 * BLOCK:(par + 1) * 2 * BLOCK]
                    logits = jnp.where(mask, sh + bias_ref[h], NEG_INF)
                    sink = sinks_ref[h] * LOG2E
                    m = jnp.maximum(jnp.max(logits, axis=-1, keepdims=True), sink)
                    p_ref[j * BLOCK:(j + 1) * BLOCK,
                          par * 2 * BLOCK:(par + 1) * 2 * BLOCK] = jnp.exp2(logits - m).astype(BF16)
                    sink_p[(j, par)] = jnp.exp2(sink - m)
            return sink_p

        def attend(qb, g, sink_p):
            r0 = qb * BLOCK
            c0 = g * LANE_TILES_PER_GROUP * LANES
            vcat = jnp.concatenate(
                [jnp.concatenate([vp_ref[2 * g, r0:r0 + 2 * BLOCK, :],
                                  vp_ref[2 * g + 1, r0:r0 + 2 * BLOCK, :]], axis=0),
                 ones_ref[...]], axis=1)
            o = jnp.dot(p_ref[...], vcat, preferred_element_type=F32)
            for j in range(LANE_TILES_PER_GROUP):
                cj = c0 + j * LANES
                oj = o[j * BLOCK:(j + 1) * BLOCK, :]
                denom = oj[:, LANES:] + jnp.where(first_half_b, sink_p[(j, 0)], sink_p[(j, 1)])
                ain_ref[r0:r0 + BLOCK, cj:cj + LANES] = (
                    oj[:, 0:LANES] / denom * sg_ref[r0:r0 + BLOCK, cj:cj + LANES]).astype(BF16)

        pool = {"halo": halo_ref[...]}

        def pool_window_sums(c):
            rs = slice(c * POOL_ROWS, (c + 1) * POOL_ROWS)
            pin = h_ref[rs, 0:D_MODEL]
            pin_b = pin.astype(BF16)
            tpos = (i * tm + c * POOL_ROWS + 1
                    + lax.broadcasted_iota(jnp.int32, (POOL_ROWS, 1), 0)).astype(F32)
            pooled = []
            for gi, w in enumerate(POOL_WINDOWS):
                cs = slice(gi * POOL_GROUP_WIDTH, (gi + 1) * POOL_GROUP_WIDTH)
                wsum = jnp.dot(bsum_ref[gi], pin_b[:, cs], preferred_element_type=F32)
                head = wsum[0:POOL_HALO] + jnp.dot(bhalo_ref[gi], pool["halo"][:, cs],
                                                   preferred_element_type=F32)
                wsum = jnp.concatenate([head, wsum[POOL_HALO:]], axis=0)
                pooled.append((wsum / jnp.minimum(tpos, float(w)) - pin[:, cs]).astype(BF16))
            pool["halo"] = pin_b[POOL_ROWS - POOL_HALO:, :]
            pool[("pooled", c)] = pooled

        def pool_mix(c):
            rs = slice(c * POOL_ROWS, (c + 1) * POOL_ROWS)
            pooled = pool.pop(("pooled", c))
            mixed = [jnp.dot(pooled[gi],
                             w_mix_ref[gi * POOL_GROUP_WIDTH:(gi + 1) * POOL_GROUP_WIDTH, :],
                             preferred_element_type=F32)
                     for gi in range(len(POOL_WINDOWS))]
            pool[("pp", c)] = (jnp.concatenate(mixed, axis=1)
                               * h_ref[rs, D_MODEL:2 * D_MODEL]).astype(BF16)

        def pool_project(c):
            pool[("p_proj", c)] = jnp.dot(pool.pop(("pp", c)), w_pp_ref[...],
                                          preferred_element_type=F32)

        def attn_project(c):
            rs = slice(c * POOL_ROWS, (c + 1) * POOL_ROWS)
            pool[("a_proj", c)] = jnp.dot(ain_ref[rs, :], w_ap_ref[...], preferred_element_type=F32)

        def merge(c):
            rs = slice(c * POOL_ROWS, (c + 1) * POOL_ROWS)
            yin_ref[rs, :] = (h_ref[rs, 2 * D_MODEL:3 * D_MODEL] * pool.pop(("a_proj", c))
                              + h_ref[rs, 3 * D_MODEL:4 * D_MODEL] * pool.pop(("p_proj", c))
                              ).astype(BF16)

        extra = {it: [] for it in range(n_iter)}
        pin_done = n_iter // 4
        gate_done = n_iter // 2
        for c in range(n_chunks):
            extra[pin_done].append(functools.partial(pool_window_sums, c))
            extra[pin_done + 1].append(functools.partial(pool_mix, c))
            extra[min(gate_done + c, n_iter - 1)].append(functools.partial(pool_project, c))
        for c in range(n_chunks - 1):
            extra[min(4 * c + 7, n_iter - 1)].append(functools.partial(attn_project, c))

        for qb in range(nblk):
            for g in range(N_KV_HEADS):
                it = qb * N_KV_HEADS + g
                last = it == n_iter - 1
                s = qk_logits(qb, g)
                if not last:
                    deferred_projection(it)
                for fn in extra[it]:
                    fn()
                sink_p = softmax_numerators(qb, g, s)
                attend(qb, g, sink_p)
                if last:
                    deferred_projection(it)
        attn_project(n_chunks - 1)

        kp_ref[:, 0:BLOCK, :] = kp_ref[:, tm:tm + BLOCK, :]
        vp_ref[:, 0:BLOCK, :] = vp_ref[:, tm:tm + BLOCK, :]
        halo_ref[...] = pool["halo"]
        for c in range(n_chunks):
            merge(c)

    @pl.when(i < n_tiles)
    def _():
        finish_previous_tile()
        tile_head()

    @pl.when(jnp.logical_and(i >= 0, i < n_tiles))
    def _():
        tile_rest()

    @pl.when(i == n_tiles)
    def _():
        finish_previous_tile()


def _pool_band_matrices():
    tm = POOL_ROWS
    t = np.arange(tm)[:, None]
    own = np.stack([((t - np.arange(tm)[None, :] >= 0) & (t - np.arange(tm)[None, :] < w))
                    for w in POOL_WINDOWS])
    th = np.arange(POOL_HALO)[:, None]
    back = th - (np.arange(POOL_HALO)[None, :] - POOL_HALO)
    halo = np.stack([back < w for w in POOL_WINDOWS])
    return own.astype(np.float32), halo.astype(np.float32)


def _layer(layer, x, mod, sinks, bias, w_in, w_ap, w_pp, w_out, w_mix, pscale, lng, lnb, *, tm):
    batch, seq, _ = x.shape
    n_tiles = seq // tm
    n_groups = len(POOL_WINDOWS)
    bsum, bhalo = _pool_band_matrices()
    rows = np.arange(HEADS_PER_LANE_TILE * 2 * BLOCK)[:, None] // (2 * BLOCK)
    ones_pat = (rows == np.arange(LANES)[None, :] // HEAD_DIM).astype(np.float32)

    def const_spec(shape):
        zeros = (0,) * len(shape)
        return pl.BlockSpec(shape, lambda b, i: zeros, pipeline_mode=pl.Buffered(1))

    def layer_spec(shape):
        idx = (layer,) + (0,) * len(shape)
        return pl.BlockSpec((1,) + shape, lambda b, i: idx, pipeline_mode=pl.Buffered(1))

    hbm_spec = pl.BlockSpec(memory_space=pl.ANY)

    return pl.pallas_call(
        functools.partial(_layer_kernel, tm=tm, layer=layer),
        grid=(batch, n_tiles + 1),
        in_specs=[
            pl.BlockSpec((1, tm, D_MODEL), lambda b, i: (b, jnp.minimum(i, n_tiles - 1), 0)),
            pl.BlockSpec((1, 3, D_MODEL), lambda b, i: (b, 0, 0)),
            pl.BlockSpec(memory_space=pltpu.SMEM),
            const_spec((N_Q_HEADS, BLOCK, 2 * BLOCK)),
            const_spec((n_groups, POOL_ROWS, POOL_ROWS)),
            const_spec((n_groups, POOL_HALO, POOL_HALO)),
            const_spec(ones_pat.shape),
            hbm_spec, hbm_spec, hbm_spec, hbm_spec, hbm_spec,
            layer_spec((1, D_MODEL)),
            layer_spec((1, D_MODEL)),
            layer_spec((1, D_MODEL)),
        ],
        out_specs=pl.BlockSpec((1, tm, D_MODEL), lambda b, i: (b, jnp.maximum(i - 1, 0), 0)),
        out_shape=jax.ShapeDtypeStruct(x.shape, F32),
        scratch_shapes=[
            pltpu.VMEM((D_MODEL, IN_WIDTH), BF16),
            pltpu.VMEM((D_MODEL, D_MODEL), BF16),
            pltpu.VMEM((D_MODEL, D_MODEL), BF16),
            pltpu.VMEM((D_MODEL, D_MODEL), BF16),
            pltpu.VMEM((n_groups * POOL_GROUP_WIDTH, POOL_GROUP_WIDTH), BF16),
            pltpu.VMEM((2, STAGE_IN_ROWS, IN_WIDTH), F32),
            pltpu.VMEM((2, STAGE_SQ_ROWS, D_MODEL), F32),
            pltpu.VMEM((n_groups * POOL_GROUP_WIDTH, POOL_GROUP_WIDTH), F32),
            pltpu.SemaphoreType.DMA((5,)),
            pltpu.VMEM((tm, D_MODEL), BF16),
            pltpu.VMEM((tm, D_MODEL), BF16),
            pltpu.VMEM((4, tm + BLOCK, LANES), BF16),
            pltpu.VMEM((4, tm + BLOCK, LANES), BF16),
            pltpu.VMEM((LANE_TILES_PER_GROUP * BLOCK, HEADS_PER_LANE_TILE * 2 * BLOCK), BF16),
            pltpu.VMEM((tm, D_MODEL), F32),
            pltpu.VMEM((tm, D_MODEL), BF16),
            pltpu.VMEM((POOL_HALO, D_MODEL), BF16),
            pltpu.VMEM((tm, IN_WIDTH - OFF_PIN), F32),
            pltpu.VMEM((tm, D_MODEL), BF16),
            pltpu.VMEM((tm, D_MODEL), F32),
        ],
        compiler_params=pltpu.CompilerParams(
            dimension_semantics=("arbitrary", "arbitrary"),
            vmem_limit_bytes=VMEM_LIMIT_BYTES),
        name="fused_layer",
    )(x, mod, sinks, bias, jnp.asarray(bsum, BF16), jnp.asarray(bhalo, BF16),
      jnp.asarray(ones_pat, BF16), w_in, w_ap, w_pp, w_out, w_mix, pscale, lng, lnb)


@functools.partial(jax.jit, static_argnames=("tm",))
def _forward(x, c, rel_bias, w_ada, b_ada, w_in, sinks, w_pool_mix, pool_scale,
             w_attn_proj, w_pool_proj, w_out, ln_gain, ln_bias, tm=512):
    batch = x.shape[0]
    mod = _adaln(c, w_ada, b_ada).reshape(DEPTH, batch, 3, D_MODEL)
    bias = _bias_table(rel_bias)
    w_mix = w_pool_mix.reshape(DEPTH, len(POOL_WINDOWS) * POOL_GROUP_WIDTH, POOL_GROUP_WIDTH)
    pscale, lng, lnb = (v.reshape(DEPTH, 1, D_MODEL) for v in (pool_scale, ln_gain, ln_bias))
    for l in range(DEPTH):
        x = _layer(l, x, mod[l], sinks[l], bias, w_in, w_attn_proj, w_pool_proj, w_out, w_mix,
                   pscale, lng, lnb, tm=tm)
    return x


def kernel(x, c, rel_bias, w_ada, b_ada, w_in, sinks, w_pool_mix, pool_scale, w_attn_proj,
           w_pool_proj, w_out, ln_gain, ln_bias):
    return _forward(x, c, rel_bias, w_ada, b_ada, w_in, sinks, w_pool_mix, pool_scale,
                    w_attn_proj, w_pool_proj, w_out, ln_gain, ln_bias)
```

```python
import functools
import math

import numpy as np
import jax
import jax.numpy as jnp
from jax import lax
from jax.experimental import pallas as pl
from jax.experimental.pallas import tpu as pltpu

D_MODEL = 1024
DEPTH = 2
N_Q_HEADS = 16
N_KV_HEADS = 2
HEAD_DIM = 64
Q_GROUP = N_Q_HEADS // N_KV_HEADS
WINDOW = 128
BLOCK = 128
NEG_INF = -1e30
NUM_BUCKETS = 32
MAX_DISTANCE = 128
POOL_WINDOWS = (2, 4, 8, 16)
POOL_GROUP_WIDTH = D_MODEL // len(POOL_WINDOWS)
POOL_HALO = 16
DEEPNORM_ALPHA = (2 * DEPTH) ** 0.25
LN_EPS = 1e-5
LOG2E = math.log2(math.e)
POOL_ROWS = 256
STAGE_IN_ROWS = 32
STAGE_SQ_ROWS = 128
LOAD_SLOTS = 4

OFF_Q = 0
OFF_KV = 1024
OFF_AGATE = 1280
OFF_PIN = 2304
OFF_PGATE = 3328
OFF_MERGE = 4352
IN_WIDTH = 6400

LANES = 128
HEADS_PER_LANE_TILE = LANES // HEAD_DIM
LANE_TILES_PER_GROUP = Q_GROUP // HEADS_PER_LANE_TILE
VMEM_LIMIT_BYTES = 56 * 1024 * 1024

F32 = jnp.float32
BF16 = jnp.bfloat16


def _bucket_map():
    q = np.arange(BLOCK)[:, None]
    k = np.arange(2 * BLOCK)[None, :]
    dist = np.maximum(q + BLOCK - k, 0)
    max_exact = NUM_BUCKETS // 2
    d = np.maximum(dist, 1).astype(np.float64)
    large = max_exact + (np.log(d / max_exact) / math.log(MAX_DISTANCE / max_exact)
                         * (NUM_BUCKETS - max_exact)).astype(np.int32)
    large = np.minimum(large, NUM_BUCKETS - 1)
    return np.where(dist < max_exact, dist, large).astype(np.int32)


def _sigmoid(x):
    return 1.0 / (1.0 + jnp.exp2(x * (-LOG2E)))


def _silu(x):
    return x * _sigmoid(x)


def _adaln_kernel(c_ref, w_ref, b_ref, o_ref):
    l = pl.program_id(0)
    sc = _silu(c_ref[...])
    o_ref[0, 0] = jnp.dot(sc, w_ref[0], preferred_element_type=F32) + b_ref[pl.ds(l, 1), :]


def _adaln(c, w_ada, b_ada):
    batch = c.shape[0]
    return pl.pallas_call(
        _adaln_kernel,
        grid=(DEPTH, 3),
        in_specs=[
            pl.BlockSpec((batch, D_MODEL), lambda l, n: (0, 0)),
            pl.BlockSpec((1, D_MODEL, D_MODEL), lambda l, n: (l, 0, n)),
            pl.BlockSpec((DEPTH, D_MODEL), lambda l, n: (0, n)),
        ],
        out_specs=pl.BlockSpec((1, 1, batch, D_MODEL), lambda l, n: (l, n, 0, 0)),
        out_shape=jax.ShapeDtypeStruct((DEPTH, 3, batch, D_MODEL), F32),
        name="adaln_mod",
    )(c, w_ada, b_ada)


BIAS_HEADS_PER_STEP = 8
BIAS_ROW_CHUNK = 32


def _bias_kernel(rel_ref, bucket_ref, o_ref):
    h0 = pl.program_id(0) * BIAS_HEADS_PER_STEP
    for r0 in range(0, BLOCK, BIAS_ROW_CHUNK):
        bucket = bucket_ref[r0:r0 + BIAS_ROW_CHUNK, :]
        acc = [jnp.zeros(bucket.shape, F32) for _ in range(BIAS_HEADS_PER_STEP)]
        for b in range(NUM_BUCKETS):
            hit = bucket == b
            for hh in range(BIAS_HEADS_PER_STEP):
                acc[hh] = jnp.where(hit, rel_ref[b, h0 + hh] * LOG2E, acc[hh])
        for hh in range(BIAS_HEADS_PER_STEP):
            o_ref[hh, r0:r0 + BIAS_ROW_CHUNK, :] = acc[hh]


def _bias_table(rel_bias):
    bucket = jnp.asarray(_bucket_map())
    return pl.pallas_call(
        _bias_kernel,
        grid=(N_Q_HEADS // BIAS_HEADS_PER_STEP,),
        in_specs=[
            pl.BlockSpec(memory_space=pltpu.SMEM),
            pl.BlockSpec((BLOCK, 2 * BLOCK), lambda h: (0, 0)),
        ],
        out_specs=pl.BlockSpec((BIAS_HEADS_PER_STEP, BLOCK, 2 * BLOCK), lambda h: (h, 0, 0)),
        out_shape=jax.ShapeDtypeStruct((N_Q_HEADS, BLOCK, 2 * BLOCK), F32),
        name="t5_bias_table",
    )(rel_bias, bucket)


def _layer_kernel(x_ref, mod_ref, sinks_ref, bias_ref, bsum_ref, bhalo_ref, ones_ref,
                  w_in_hbm, w_ap_hbm, w_pp_hbm, w_out_hbm, w_mix_hbm,
                  pscale_ref, lng_ref, lnb_ref, o_ref,
                  w_in_ref, w_ap_ref, w_pp_ref, w_out_ref, w_mix_ref,
                  stage_in_ref, stage_sq_ref, stage_mix_ref, load_sem,
                  u_ref, q_ref, kp_ref, vp_ref, p_ref, sg_ref, ain_ref, halo_ref, h_ref,
                  yin_ref, xprev_ref, *, tm, layer):
    i = pl.program_id(1)
    n_tiles = pl.num_programs(1) - 1
    nblk = tm // BLOCK
    n_chunks = tm // POOL_ROWS
    n_iter = nblk * N_KV_HEADS
    cw = (IN_WIDTH - OFF_PIN) // n_iter

    @pl.when(jnp.logical_and(pl.program_id(0) == 0, i == 0))
    def _():
        copies = []

        def add(hbm_rows, stage, sem, dst):
            copies.append((pltpu.make_async_copy(hbm_rows, stage, sem), stage, dst))

        for k in range(D_MODEL // STAGE_IN_ROWS):
            rows = slice(k * STAGE_IN_ROWS, (k + 1) * STAGE_IN_ROWS)
            slot = k % LOAD_SLOTS
            add(w_in_hbm.at[layer, rows, :], stage_in_ref.at[slot], load_sem.at[slot],
                w_in_ref.at[rows, :])
        n_sq = 0
        for hbm, dst in ((w_ap_hbm, w_ap_ref), (w_pp_hbm, w_pp_ref), (w_out_hbm, w_out_ref)):
            for k in range(D_MODEL // STAGE_SQ_ROWS):
                rows = slice(k * STAGE_SQ_ROWS, (k + 1) * STAGE_SQ_ROWS)
                slot = n_sq % LOAD_SLOTS
                n_sq += 1
                add(hbm.at[layer, rows, :], stage_sq_ref.at[slot], load_sem.at[LOAD_SLOTS + slot],
                    dst.at[rows, :])
        add(w_mix_hbm.at[layer], stage_mix_ref, load_sem.at[2 * LOAD_SLOTS], w_mix_ref)

        for k in range(min(LOAD_SLOTS, len(copies))):
            copies[k][0].start()
        for k, (copy, stage, dst) in enumerate(copies):
            copy.wait()
            dst[...] = stage[...].astype(BF16)
            if k + LOAD_SLOTS < len(copies):
                copies[k + LOAD_SLOTS][0].start()

    @pl.when(i == 0)
    def _():
        kp_ref[:, 0:BLOCK, :] = jnp.zeros((4, BLOCK, LANES), BF16)
        vp_ref[:, 0:BLOCK, :] = jnp.zeros((4, BLOCK, LANES), BF16)
        halo_ref[...] = jnp.zeros((POOL_HALO, D_MODEL), BF16)
        yin_ref[...] = jnp.zeros((tm, D_MODEL), BF16)
        xprev_ref[...] = jnp.zeros((tm, D_MODEL), F32)

    def proj(lo, hi):
        return jnp.dot(u_ref[...], w_in_ref[:, lo:hi], preferred_element_type=F32)

    def mod_row(kind):
        return mod_ref[0, kind, pl.ds(pl.program_id(0), 1), :]

    def finish_previous_tile():
        gate1 = 1.0 + mod_row(2)
        gain = lng_ref[layer:layer + 1, :]
        beta = lnb_ref[layer:layer + 1, :]
        for c in range(n_chunks):
            rs = slice(c * POOL_ROWS, (c + 1) * POOL_ROWS)
            y = jnp.dot(yin_ref[rs, :], w_out_ref[...], preferred_element_type=F32) * gate1
            r = DEEPNORM_ALPHA * xprev_ref[rs, :] + y
            mu = jnp.mean(r, axis=-1, keepdims=True)
            rc = r - mu
            var = jnp.mean(rc * rc, axis=-1, keepdims=True)
            o_ref[0, rs, :] = rc * lax.rsqrt(var + LN_EPS) * gain + beta

    def tile_head():
        x = x_ref[0]
        shift = mod_row(0)
        scale1 = 1.0 + mod_row(1)
        u_ref[...] = (x * scale1 + shift).astype(BF16)
        xprev_ref[...] = x

        kv = proj(OFF_KV, OFF_AGATE)
        lane = lax.broadcasted_iota(jnp.int32, (tm, LANES), 1)
        first_half = lane < HEAD_DIM
        for src, dst in ((kv[:, 0:LANES], kp_ref), (kv[:, LANES:2 * LANES], vp_ref)):
            g0_lo = jnp.where(first_half, src, 0.0)
            g1_hi = jnp.where(first_half, 0.0, src)
            dst[0, BLOCK:, :] = g0_lo.astype(BF16)
            dst[1, BLOCK:, :] = pltpu.roll(g0_lo, HEAD_DIM, 1).astype(BF16)
            dst[2, BLOCK:, :] = pltpu.roll(g1_hi, HEAD_DIM, 1).astype(BF16)
            dst[3, BLOCK:, :] = g1_hi.astype(BF16)
        q_ref[...] = (proj(OFF_Q, OFF_KV) * (HEAD_DIM ** -0.5 * LOG2E)).astype(BF16)

    def tile_rest():
        sg_ref[...] = _silu(proj(OFF_AGATE, OFF_PIN))

        qi = lax.broadcasted_iota(jnp.int32, (BLOCK, 2 * BLOCK), 0)
        ki = lax.broadcasted_iota(jnp.int32, (BLOCK, 2 * BLOCK), 1)
        dist = qi + BLOCK - ki
        band = (dist >= 0) & (dist < WINDOW)
        band_first = band & (ki >= jnp.where(i == 0, BLOCK, 0))
        lane_b = lax.broadcasted_iota(jnp.int32, (BLOCK, LANES), 1)
        first_half_b = lane_b < HEAD_DIM

        def deferred_projection(it):
            hv = proj(OFF_PIN + it * cw, OFF_PIN + (it + 1) * cw)
            segment, seg_off = divmod(it * cw, D_MODEL)
            if segment == 1:
                hv = _silu(hv) * pscale_ref[layer:layer + 1, seg_off:seg_off + cw]
            elif segment >= 2:
                hv = _sigmoid(hv)
            h_ref[:, it * cw:(it + 1) * cw] = hv

        def qk_logits(qb, g):
            r0 = qb * BLOCK
            c0 = g * LANE_TILES_PER_GROUP * LANES
            qs = jnp.concatenate(
                [q_ref[r0:r0 + BLOCK, c0 + j * LANES:c0 + (j + 1) * LANES]
                 for j in range(LANE_TILES_PER_GROUP)], axis=0)
            kcat = jnp.concatenate([kp_ref[2 * g, r0:r0 + 2 * BLOCK, :],
                                    kp_ref[2 * g + 1, r0:r0 + 2 * BLOCK, :]], axis=0)
            return lax.dot_general(qs, kcat, (((1,), (1,)), ((), ())), preferred_element_type=F32)

        def softmax_numerators(qb, g, s):
            mask = band_first if qb == 0 else band
            sink_p = {}
            for j in range(LANE_TILES_PER_GROUP):
                for par in range(HEADS_PER_LANE_TILE):
                    h = g * Q_GROUP + j * HEADS_PER_LANE_TILE + par
                    sh = s[j * BLOCK:(j + 1) * BLOCK, par * 2 * BLOCK:(par + 1) * 2 * BLOCK]
                    logits = jnp.where(mask, sh + bias_ref[h], NEG_INF)
                    sink = sinks_ref[layer, h] * LOG2E
                    m = jnp.maximum(jnp.max(logits, axis=-1, keepdims=True), sink)
                    p_ref[j * BLOCK:(j + 1) * BLOCK,
                          par * 2 * BLOCK:(par + 1) * 2 * BLOCK] = jnp.exp2(logits - m).astype(BF16)
                    sink_p[(j, par)] = jnp.exp2(sink - m)
            return sink_p

        def attend(qb, g, sink_p):
            r0 = qb * BLOCK
            c0 = g * LANE_TILES_PER_GROUP * LANES
            vcat = jnp.concatenate(
                [jnp.concatenate([vp_ref[2 * g, r0:r0 + 2 * BLOCK, :],
                                  vp_ref[2 * g + 1, r0:r0 + 2 * BLOCK, :]], axis=0),
                 ones_ref[...]], axis=1)
            o = jnp.dot(p_ref[...], vcat, preferred_element_type=F32)
            for j in range(LANE_TILES_PER_GROUP):
                cj = c0 + j * LANES
                oj = o[j * BLOCK:(j + 1) * BLOCK, :]
                denom = oj[:, LANES:] + jnp.where(first_half_b, sink_p[(j, 0)], sink_p[(j, 1)])
                ain_ref[r0:r0 + BLOCK, cj:cj + LANES] = (
                    oj[:, 0:LANES] / denom * sg_ref[r0:r0 + BLOCK, cj:cj + LANES]).astype(BF16)

        pool = {"halo": halo_ref[...]}

        def pool_window_sums(c):
            rs = slice(c * POOL_ROWS, (c + 1) * POOL_ROWS)
            pin = h_ref[rs, 0:D_MODEL]
            pin_b = pin.astype(BF16)
            tpos = (i * tm + c * POOL_ROWS + 1
                    + lax.broadcasted_iota(jnp.int32, (POOL_ROWS, 1), 0)).astype(F32)
            pooled = []
            for gi, w in enumerate(POOL_WINDOWS):
                cs = slice(gi * POOL_GROUP_WIDTH, (gi + 1) * POOL_GROUP_WIDTH)
                wsum = jnp.dot(bsum_ref[gi], pin_b[:, cs], preferred_element_type=F32)
                head = wsum[0:POOL_HALO] + jnp.dot(bhalo_ref[gi], pool["halo"][:, cs],
                                                   preferred_element_type=F32)
                wsum = jnp.concatenate([head, wsum[POOL_HALO:]], axis=0)
                pooled.append((wsum / jnp.minimum(tpos, float(w)) - pin[:, cs]).astype(BF16))
            pool["halo"] = pin_b[POOL_ROWS - POOL_HALO:, :]
            pool[("pooled", c)] = pooled

        def pool_mix(c):
            rs = slice(c * POOL_ROWS, (c + 1) * POOL_ROWS)
            pooled = pool.pop(("pooled", c))
            mixed = [jnp.dot(pooled[gi],
                             w_mix_ref[gi * POOL_GROUP_WIDTH:(gi + 1) * POOL_GROUP_WIDTH, :],
                             preferred_element_type=F32)
                     for gi in range(len(POOL_WINDOWS))]
            pool[("pp", c)] = (jnp.concatenate(mixed, axis=1)
                               * h_ref[rs, D_MODEL:2 * D_MODEL]).astype(BF16)

        def pool_project(c):
            pool[("p_proj", c)] = jnp.dot(pool.pop(("pp", c)), w_pp_ref[...],
                                          preferred_element_type=F32)

        def attn_project(c):
            rs = slice(c * POOL_ROWS, (c + 1) * POOL_ROWS)
            pool[("a_proj", c)] = jnp.dot(ain_ref[rs, :], w_ap_ref[...], preferred_element_type=F32)

        def merge(c):
            rs = slice(c * POOL_ROWS, (c + 1) * POOL_ROWS)
            yin_ref[rs, :] = (h_ref[rs, 2 * D_MODEL:3 * D_MODEL] * pool.pop(("a_proj", c))
                              + h_ref[rs, 3 * D_MODEL:4 * D_MODEL] * pool.pop(("p_proj", c))
                              ).astype(BF16)

        extra = {it: [] for it in range(n_iter)}
        pin_done = n_iter // 4
        gate_done = n_iter // 2
        for c in range(n_chunks):
            extra[pin_done].append(functools.partial(pool_window_sums, c))
            extra[pin_done + 1].append(functools.partial(pool_mix, c))
            extra[min(gate_done + c, n_iter - 1)].append(functools.partial(pool_project, c))
        for c in range(n_chunks - 1):
            extra[min(4 * c + 7, n_iter - 1)].append(functools.partial(attn_project, c))

        for qb in range(nblk):
            for g in range(N_KV_HEADS):
                it = qb * N_KV_HEADS + g
                last = it == n_iter - 1
                s = qk_logits(qb, g)
                if not last:
                    deferred_projection(it)
                for fn in extra[it]:
                    fn()
                sink_p = softmax_numerators(qb, g, s)
                attend(qb, g, sink_p)
                if last:
                    deferred_projection(it)
        attn_project(n_chunks - 1)

        kp_ref[:, 0:BLOCK, :] = kp_ref[:, tm:tm + BLOCK, :]
        vp_ref[:, 0:BLOCK, :] = vp_ref[:, tm:tm + BLOCK, :]
        halo_ref[...] = pool["halo"]
        for c in range(n_chunks):
            merge(c)

    @pl.when(i < n_tiles)
    def _():
        finish_previous_tile()
        tile_head()

    @pl.when(jnp.logical_and(i >= 0, i < n_tiles))
    def _():
        tile_rest()

    @pl.when(i == n_tiles)
    def _():
        finish_previous_tile()


def _pool_band_matrices():
    tm = POOL_ROWS
    t = np.arange(tm)[:, None]
    own = np.stack([((t - np.arange(tm)[None, :] >= 0) & (t - np.arange(tm)[None, :] < w))
                    for w in POOL_WINDOWS])
    th = np.arange(POOL_HALO)[:, None]
    back = th - (np.arange(POOL_HALO)[None, :] - POOL_HALO)
    halo = np.stack([back < w for w in POOL_WINDOWS])
    return own.astype(np.float32), halo.astype(np.float32)


def _layer(layer, x, mod, sinks, bias, w_in, w_ap, w_pp, w_out, w_mix, pscale, lng, lnb, *, tm):
    batch, seq, _ = x.shape
    n_tiles = seq // tm
    n_groups = len(POOL_WINDOWS)
    bsum, bhalo = _pool_band_matrices()
    rows = np.arange(HEADS_PER_LANE_TILE * 2 * BLOCK)[:, None] // (2 * BLOCK)
    ones_pat = (rows == np.arange(LANES)[None, :] // HEAD_DIM).astype(np.float32)

    def const_spec(shape):
        zeros = (0,) * len(shape)
        return pl.BlockSpec(shape, lambda b, i: zeros, pipeline_mode=pl.Buffered(1))

    hbm_spec = pl.BlockSpec(memory_space=pl.ANY)

    return pl.pallas_call(
        functools.partial(_layer_kernel, tm=tm, layer=layer),
        grid=(batch, n_tiles + 1),
        in_specs=[
            pl.BlockSpec((1, tm, D_MODEL), lambda b, i: (b, jnp.minimum(i, n_tiles - 1), 0)),
            pl.BlockSpec((1, 3, batch, D_MODEL), lambda b, i: (layer, 0, 0, 0),
                         pipeline_mode=pl.Buffered(1)),
            pl.BlockSpec(memory_space=pltpu.SMEM),
            const_spec((N_Q_HEADS, BLOCK, 2 * BLOCK)),
            const_spec((n_groups, POOL_ROWS, POOL_ROWS)),
            const_spec((n_groups, POOL_HALO, POOL_HALO)),
            const_spec(ones_pat.shape),
            hbm_spec, hbm_spec, hbm_spec, hbm_spec, hbm_spec,
            const_spec((DEPTH, D_MODEL)),
            const_spec((DEPTH, D_MODEL)),
            const_spec((DEPTH, D_MODEL)),
        ],
        out_specs=pl.BlockSpec((1, tm, D_MODEL), lambda b, i: (b, jnp.maximum(i - 1, 0), 0)),
        out_shape=jax.ShapeDtypeStruct(x.shape, F32),
        scratch_shapes=[
            pltpu.VMEM((D_MODEL, IN_WIDTH), BF16),
            pltpu.VMEM((D_MODEL, D_MODEL), BF16),
            pltpu.VMEM((D_MODEL, D_MODEL), BF16),
            pltpu.VMEM((D_MODEL, D_MODEL), BF16),
            pltpu.VMEM((n_groups * POOL_GROUP_WIDTH, POOL_GROUP_WIDTH), BF16),
            pltpu.VMEM((LOAD_SLOTS, STAGE_IN_ROWS, IN_WIDTH), F32),
            pltpu.VMEM((LOAD_SLOTS, STAGE_SQ_ROWS, D_MODEL), F32),
            pltpu.VMEM((n_groups * POOL_GROUP_WIDTH, POOL_GROUP_WIDTH), F32),
            pltpu.SemaphoreType.DMA((2 * LOAD_SLOTS + 1,)),
            pltpu.VMEM((tm, D_MODEL), BF16),
            pltpu.VMEM((tm, D_MODEL), BF16),
            pltpu.VMEM((4, tm + BLOCK, LANES), BF16),
            pltpu.VMEM((4, tm + BLOCK, LANES), BF16),
            pltpu.VMEM((LANE_TILES_PER_GROUP * BLOCK, HEADS_PER_LANE_TILE * 2 * BLOCK), BF16),
            pltpu.VMEM((tm, D_MODEL), F32),
            pltpu.VMEM((tm, D_MODEL), BF16),
            pltpu.VMEM((POOL_HALO, D_MODEL), BF16),
            pltpu.VMEM((tm, IN_WIDTH - OFF_PIN), F32),
            pltpu.VMEM((tm, D_MODEL), BF16),
            pltpu.VMEM((tm, D_MODEL), F32),
        ],
        compiler_params=pltpu.CompilerParams(
            dimension_semantics=("arbitrary", "arbitrary"),
            vmem_limit_bytes=VMEM_LIMIT_BYTES),
        name="fused_layer",
    )(x, mod, sinks, bias, jnp.asarray(bsum, BF16), jnp.asarray(bhalo, BF16),
      jnp.asarray(ones_pat, BF16), w_in, w_ap, w_pp, w_out, w_mix, pscale, lng, lnb)


@functools.partial(jax.jit, static_argnames=("tm",))
def _forward(x, c, rel_bias, w_ada, b_ada, w_in, sinks, w_pool_mix, pool_scale,
             w_attn_proj, w_pool_proj, w_out, ln_gain, ln_bias, tm=512):
    batch = x.shape[0]
    mod = _adaln(c, w_ada, b_ada)
    bias = _bias_table(rel_bias)
    w_mix = w_pool_mix.reshape(DEPTH, len(POOL_WINDOWS) * POOL_GROUP_WIDTH, POOL_GROUP_WIDTH)
    for l in range(DEPTH):
        x = _layer(l, x, mod, sinks, bias, w_in, w_attn_proj, w_pool_proj, w_out, w_mix,
                   pool_scale, ln_gain, ln_bias, tm=tm)
    return x


def kernel(x, c, rel_bias, w_ada, b_ada, w_in, sinks, w_pool_mix, pool_scale, w_attn_proj,
           w_pool_proj, w_out, ln_gain, ln_bias):
    return _forward(x, c, rel_bias, w_ada, b_ada, w_in, sinks, w_pool_mix, pool_scale,
                    w_attn_proj, w_pool_proj, w_out, ln_gain, ln_bias)
```

```python
import functools
import math

import numpy as np
import jax
import jax.numpy as jnp
from jax import lax
from jax.experimental import pallas as pl
from jax.experimental.pallas import tpu as pltpu

D_MODEL = 1024
DEPTH = 2
N_Q_HEADS = 16
N_KV_HEADS = 2
HEAD_DIM = 64
Q_GROUP = N_Q_HEADS // N_KV_HEADS
WINDOW = 128
BLOCK = 128
NEG_INF = -1e30
NUM_BUCKETS = 32
MAX_DISTANCE = 128
POOL_WINDOWS = (2, 4, 8, 16)
POOL_GROUP_WIDTH = D_MODEL // len(POOL_WINDOWS)
POOL_HALO = 16
DEEPNORM_ALPHA = (2 * DEPTH) ** 0.25
LN_EPS = 1e-5
LOG2E = math.log2(math.e)
POOL_ROWS = 256
STAGE_IN_ROWS = 32
STAGE_SQ_ROWS = 128
LOAD_SLOTS = 4

OFF_Q = 0
OFF_KV = 1024
OFF_AGATE = 1280
OFF_PIN = 2304
OFF_PGATE = 3328
OFF_MERGE = 4352
IN_WIDTH = 6400

LANES = 128
HEADS_PER_LANE_TILE = LANES // HEAD_DIM
LANE_TILES_PER_GROUP = Q_GROUP // HEADS_PER_LANE_TILE
VMEM_LIMIT_BYTES = 60 * 1024 * 1024

F32 = jnp.float32
BF16 = jnp.bfloat16


def _bucket_map():
    q = np.arange(BLOCK)[:, None]
    k = np.arange(2 * BLOCK)[None, :]
    dist = np.maximum(q + BLOCK - k, 0)
    max_exact = NUM_BUCKETS // 2
    d = np.maximum(dist, 1).astype(np.float64)
    large = max_exact + (np.log(d / max_exact) / math.log(MAX_DISTANCE / max_exact)
                         * (NUM_BUCKETS - max_exact)).astype(np.int32)
    large = np.minimum(large, NUM_BUCKETS - 1)
    return np.where(dist < max_exact, dist, large).astype(np.int32)


def _sigmoid(x):
    return 1.0 / (1.0 + jnp.exp2(x * (-LOG2E)))


def _silu(x):
    return x * _sigmoid(x)


def _adaln_kernel(c_ref, w_ref, b_ref, o_ref):
    l = pl.program_id(0)
    sc = _silu(c_ref[...])
    o_ref[0, 0] = jnp.dot(sc, w_ref[0], preferred_element_type=F32) + b_ref[pl.ds(l, 1), :]


def _adaln(c, w_ada, b_ada):
    batch = c.shape[0]
    return pl.pallas_call(
        _adaln_kernel,
        grid=(DEPTH, 3),
        in_specs=[
            pl.BlockSpec((batch, D_MODEL), lambda l, n: (0, 0)),
            pl.BlockSpec((1, D_MODEL, D_MODEL), lambda l, n: (l, 0, n)),
            pl.BlockSpec((DEPTH, D_MODEL), lambda l, n: (0, n)),
        ],
        out_specs=pl.BlockSpec((1, 1, batch, D_MODEL), lambda l, n: (l, n, 0, 0)),
        out_shape=jax.ShapeDtypeStruct((DEPTH, 3, batch, D_MODEL), F32),
        name="adaln_mod",
    )(c, w_ada, b_ada)


BIAS_HEADS_PER_STEP = 8
BIAS_ROW_CHUNK = 32


def _bias_table_piece(rel_ref, bucket_ref, bias_ref, piece):
    pieces_per_head_group = BLOCK // BIAS_ROW_CHUNK
    h0 = (piece // pieces_per_head_group) * BIAS_HEADS_PER_STEP
    r0 = (piece % pieces_per_head_group) * BIAS_ROW_CHUNK
    bucket = bucket_ref[r0:r0 + BIAS_ROW_CHUNK, :]
    acc = [jnp.zeros(bucket.shape, F32) for _ in range(BIAS_HEADS_PER_STEP)]
    for b in range(NUM_BUCKETS):
        hit = bucket == b
        for hh in range(BIAS_HEADS_PER_STEP):
            acc[hh] = jnp.where(hit, rel_ref[b, h0 + hh] * LOG2E, acc[hh])
    for hh in range(BIAS_HEADS_PER_STEP):
        bias_ref[h0 + hh, r0:r0 + BIAS_ROW_CHUNK, :] = acc[hh]


BIAS_PIECES = (N_Q_HEADS // BIAS_HEADS_PER_STEP) * (BLOCK // BIAS_ROW_CHUNK)


def _layer_kernel(x_ref, mod_ref, sinks_ref, rel_ref, bucket_ref, bsum_ref, bhalo_ref, ones_ref,
                  w_in_hbm, w_ap_hbm, w_pp_hbm, w_out_hbm, w_mix_hbm,
                  pscale_ref, lng_ref, lnb_ref, o_ref,
                  bias_ref, w_in_ref, w_ap_ref, w_pp_ref, w_out_ref, w_mix_ref,
                  stage_in_ref, stage_sq_ref, stage_mix_ref, load_sem,
                  u_ref, q_ref, kp_ref, vp_ref, p_ref, sg_ref, ain_ref, halo_ref, h_ref,
                  yin_ref, xprev_ref, *, tm, layer):
    i = pl.program_id(1)
    n_tiles = pl.num_programs(1) - 1
    nblk = tm // BLOCK
    n_chunks = tm // POOL_ROWS
    n_iter = nblk * N_KV_HEADS
    cw = (IN_WIDTH - OFF_PIN) // n_iter

    @pl.when(jnp.logical_and(pl.program_id(0) == 0, i == 0))
    def _():
        copies = []

        def add(hbm_rows, stage, sem, dst):
            copies.append((pltpu.make_async_copy(hbm_rows, stage, sem), stage, dst))

        for k in range(D_MODEL // STAGE_IN_ROWS):
            rows = slice(k * STAGE_IN_ROWS, (k + 1) * STAGE_IN_ROWS)
            slot = k % LOAD_SLOTS
            add(w_in_hbm.at[layer, rows, :], stage_in_ref.at[slot], load_sem.at[slot],
                w_in_ref.at[rows, :])
        n_sq = 0
        for hbm, dst in ((w_ap_hbm, w_ap_ref), (w_pp_hbm, w_pp_ref), (w_out_hbm, w_out_ref)):
            for k in range(D_MODEL // STAGE_SQ_ROWS):
                rows = slice(k * STAGE_SQ_ROWS, (k + 1) * STAGE_SQ_ROWS)
                slot = n_sq % LOAD_SLOTS
                n_sq += 1
                add(hbm.at[layer, rows, :], stage_sq_ref.at[slot], load_sem.at[LOAD_SLOTS + slot],
                    dst.at[rows, :])
        add(w_mix_hbm.at[layer], stage_mix_ref, load_sem.at[2 * LOAD_SLOTS], w_mix_ref)

        for k in range(min(LOAD_SLOTS, len(copies))):
            copies[k][0].start()
        bias_every = len(copies) // BIAS_PIECES
        for k, (copy, stage, dst) in enumerate(copies):
            copy.wait()
            dst[...] = stage[...].astype(BF16)
            if k + LOAD_SLOTS < len(copies):
                copies[k + LOAD_SLOTS][0].start()
            if k % bias_every == 0 and k // bias_every < BIAS_PIECES:
                _bias_table_piece(rel_ref, bucket_ref, bias_ref, k // bias_every)

    def mod_row(kind):
        return mod_ref[0, kind, pl.ds(pl.program_id(0), 1), :]

    @pl.when(i == 0)
    def _():
        kp_ref[:, 0:BLOCK, :] = jnp.zeros((4, BLOCK, LANES), BF16)
        vp_ref[:, 0:BLOCK, :] = jnp.zeros((4, BLOCK, LANES), BF16)
        halo_ref[...] = jnp.zeros((POOL_HALO, D_MODEL), BF16)
        yin_ref[...] = jnp.zeros((tm, D_MODEL), BF16)
        xprev_ref[...] = jnp.zeros((tm, D_MODEL), F32)

    def proj(lo, hi):
        return jnp.dot(u_ref[...], w_in_ref[:, lo:hi], preferred_element_type=F32)

    def finish_previous_tile():
        gate1 = 1.0 + mod_row(2)
        gain = lng_ref[layer:layer + 1, :]
        beta = lnb_ref[layer:layer + 1, :]
        for c in range(n_chunks):
            rs = slice(c * POOL_ROWS, (c + 1) * POOL_ROWS)
            y = jnp.dot(yin_ref[rs, :], w_out_ref[...], preferred_element_type=F32) * gate1
            r = DEEPNORM_ALPHA * xprev_ref[rs, :] + y
            mu = jnp.mean(r, axis=-1, keepdims=True)
            rc = r - mu
            var = jnp.mean(rc * rc, axis=-1, keepdims=True)
            o_ref[0, rs, :] = rc * lax.rsqrt(var + LN_EPS) * gain + beta

    def tile_head():
        x = x_ref[0]
        u_ref[...] = (x * (1.0 + mod_row(1)) + mod_row(0)).astype(BF16)
        xprev_ref[...] = x

        kv = proj(OFF_KV, OFF_AGATE)
        lane = lax.broadcasted_iota(jnp.int32, (tm, LANES), 1)
        first_half = lane < HEAD_DIM
        for src, dst in ((kv[:, 0:LANES], kp_ref), (kv[:, LANES:2 * LANES], vp_ref)):
            g0_lo = jnp.where(first_half, src, 0.0)
            g1_hi = jnp.where(first_half, 0.0, src)
            dst[0, BLOCK:, :] = g0_lo.astype(BF16)
            dst[1, BLOCK:, :] = pltpu.roll(g0_lo, HEAD_DIM, 1).astype(BF16)
            dst[2, BLOCK:, :] = pltpu.roll(g1_hi, HEAD_DIM, 1).astype(BF16)
            dst[3, BLOCK:, :] = g1_hi.astype(BF16)
        q_ref[...] = (proj(OFF_Q, OFF_KV) * (HEAD_DIM ** -0.5 * LOG2E)).astype(BF16)

    def tile_rest():
        sg_ref[...] = _silu(proj(OFF_AGATE, OFF_PIN))

        qi = lax.broadcasted_iota(jnp.int32, (BLOCK, 2 * BLOCK), 0)
        ki = lax.broadcasted_iota(jnp.int32, (BLOCK, 2 * BLOCK), 1)
        dist = qi + BLOCK - ki
        band = (dist >= 0) & (dist < WINDOW)
        band_first = band & (ki >= jnp.where(i == 0, BLOCK, 0))
        lane_b = lax.broadcasted_iota(jnp.int32, (BLOCK, LANES), 1)
        first_half_b = lane_b < HEAD_DIM

        def deferred_projection(it):
            hv = proj(OFF_PIN + it * cw, OFF_PIN + (it + 1) * cw)
            segment, seg_off = divmod(it * cw, D_MODEL)
            if segment == 1:
                hv = _silu(hv) * pscale_ref[layer:layer + 1, seg_off:seg_off + cw]
            elif segment >= 2:
                hv = _sigmoid(hv)
            h_ref[:, it * cw:(it + 1) * cw] = hv

        def qk_logits(qb, g):
            r0 = qb * BLOCK
            c0 = g * LANE_TILES_PER_GROUP * LANES
            qs = jnp.concatenate(
                [q_ref[r0:r0 + BLOCK, c0 + j * LANES:c0 + (j + 1) * LANES]
                 for j in range(LANE_TILES_PER_GROUP)], axis=0)
            kcat = jnp.concatenate([kp_ref[2 * g, r0:r0 + 2 * BLOCK, :],
                                    kp_ref[2 * g + 1, r0:r0 + 2 * BLOCK, :]], axis=0)
            return lax.dot_general(qs, kcat, (((1,), (1,)), ((), ())), preferred_element_type=F32)

        def softmax_numerators(qb, g, s):
            mask = band_first if qb == 0 else band
            sink_p = {}
            for j in range(LANE_TILES_PER_GROUP):
                for par in range(HEADS_PER_LANE_TILE):
                    h = g * Q_GROUP + j * HEADS_PER_LANE_TILE + par
                    sh = s[j * BLOCK:(j + 1) * BLOCK, par * 2 * BLOCK:(par + 1) * 2 * BLOCK]
                    logits = jnp.where(mask, sh + bias_ref[h], NEG_INF)
                    sink = sinks_ref[layer, h] * LOG2E
                    m = jnp.maximum(jnp.max(logits, axis=-1, keepdims=True), sink)
                    p_ref[j * BLOCK:(j + 1) * BLOCK,
                          par * 2 * BLOCK:(par + 1) * 2 * BLOCK] = jnp.exp2(logits - m).astype(BF16)
                    sink_p[(j, par)] = jnp.exp2(sink - m)
            return sink_p

        def attend(qb, g, sink_p):
            r0 = qb * BLOCK
            c0 = g * LANE_TILES_PER_GROUP * LANES
            vcat = jnp.concatenate(
                [jnp.concatenate([vp_ref[2 * g, r0:r0 + 2 * BLOCK, :],
                                  vp_ref[2 * g + 1, r0:r0 + 2 * BLOCK, :]], axis=0),
                 ones_ref[...]], axis=1)
            o = jnp.dot(p_ref[...], vcat, preferred_element_type=F32)
            for j in range(LANE_TILES_PER_GROUP):
                cj = c0 + j * LANES
                oj = o[j * BLOCK:(j + 1) * BLOCK, :]
                denom = oj[:, LANES:] + jnp.where(first_half_b, sink_p[(j, 0)], sink_p[(j, 1)])
                ain_ref[r0:r0 + BLOCK, cj:cj + LANES] = (
                    oj[:, 0:LANES] / denom * sg_ref[r0:r0 + BLOCK, cj:cj + LANES]).astype(BF16)

        pool = {"halo": halo_ref[...]}

        def pool_window_sums(c):
            rs = slice(c * POOL_ROWS, (c + 1) * POOL_ROWS)
            pin = h_ref[rs, 0:D_MODEL]
            pin_b = pin.astype(BF16)
            tpos = (i * tm + c * POOL_ROWS + 1
                    + lax.broadcasted_iota(jnp.int32, (POOL_ROWS, 1), 0)).astype(F32)
            pooled = []
            for gi, w in enumerate(POOL_WINDOWS):
                cs = slice(gi * POOL_GROUP_WIDTH, (gi + 1) * POOL_GROUP_WIDTH)
                wsum = jnp.dot(bsum_ref[gi], pin_b[:, cs], preferred_element_type=F32)
                head = wsum[0:POOL_HALO] + jnp.dot(bhalo_ref[gi], pool["halo"][:, cs],
                                                   preferred_element_type=F32)
                wsum = jnp.concatenate([head, wsum[POOL_HALO:]], axis=0)
                pooled.append((wsum / jnp.minimum(tpos, float(w)) - pin[:, cs]).astype(BF16))
            pool["halo"] = pin_b[POOL_ROWS - POOL_HALO:, :]
            pool[("pooled", c)] = pooled

        def pool_mix(c):
            rs = slice(c * POOL_ROWS, (c + 1) * POOL_ROWS)
            pooled = pool.pop(("pooled", c))
            mixed = [jnp.dot(pooled[gi],
                             w_mix_ref[gi * POOL_GROUP_WIDTH:(gi + 1) * POOL_GROUP_WIDTH, :],
                             preferred_element_type=F32)
                     for gi in range(len(POOL_WINDOWS))]
            pool[("pp", c)] = (jnp.concatenate(mixed, axis=1)
                               * h_ref[rs, D_MODEL:2 * D_MODEL]).astype(BF16)

        def pool_project(c):
            pool[("p_proj", c)] = jnp.dot(pool.pop(("pp", c)), w_pp_ref[...],
                                          preferred_element_type=F32)

        def attn_project(c):
            rs = slice(c * POOL_ROWS, (c + 1) * POOL_ROWS)
            pool[("a_proj", c)] = jnp.dot(ain_ref[rs, :], w_ap_ref[...], preferred_element_type=F32)

        def merge(c):
            rs = slice(c * POOL_ROWS, (c + 1) * POOL_ROWS)
            yin_ref[rs, :] = (h_ref[rs, 2 * D_MODEL:3 * D_MODEL] * pool.pop(("a_proj", c))
                              + h_ref[rs, 3 * D_MODEL:4 * D_MODEL] * pool.pop(("p_proj", c))
                              ).astype(BF16)

        extra = {it: [] for it in range(n_iter)}
        pin_done = n_iter // 4
        gate_done = n_iter // 2
        for c in range(n_chunks):
            extra[pin_done].append(functools.partial(pool_window_sums, c))
            extra[pin_done + 1].append(functools.partial(pool_mix, c))
            extra[min(gate_done + c, n_iter - 1)].append(functools.partial(pool_project, c))
        for c in range(n_chunks - 1):
            extra[min(4 * c + 7, n_iter - 1)].append(functools.partial(attn_project, c))

        for qb in range(nblk):
            for g in range(N_KV_HEADS):
                it = qb * N_KV_HEADS + g
                last = it == n_iter - 1
                s = qk_logits(qb, g)
                if not last:
                    deferred_projection(it)
                for fn in extra[it]:
                    fn()
                sink_p = softmax_numerators(qb, g, s)
                attend(qb, g, sink_p)
                if last:
                    deferred_projection(it)
        attn_project(n_chunks - 1)

        kp_ref[:, 0:BLOCK, :] = kp_ref[:, tm:tm + BLOCK, :]
        vp_ref[:, 0:BLOCK, :] = vp_ref[:, tm:tm + BLOCK, :]
        halo_ref[...] = pool["halo"]
        for c in range(n_chunks):
            merge(c)

    @pl.when(i < n_tiles)
    def _():
        finish_previous_tile()
        tile_head()

    @pl.when(jnp.logical_and(i >= 0, i < n_tiles))
    def _():
        tile_rest()

    @pl.when(i == n_tiles)
    def _():
        finish_previous_tile()


def _pool_band_matrices():
    tm = POOL_ROWS
    t = np.arange(tm)[:, None]
    own = np.stack([((t - np.arange(tm)[None, :] >= 0) & (t - np.arange(tm)[None, :] < w))
                    for w in POOL_WINDOWS])
    th = np.arange(POOL_HALO)[:, None]
    back = th - (np.arange(POOL_HALO)[None, :] - POOL_HALO)
    halo = np.stack([back < w for w in POOL_WINDOWS])
    return own.astype(np.float32), halo.astype(np.float32)


def _layer(layer, x, mod, sinks, rel_bias, w_in, w_ap, w_pp, w_out, w_mix, pscale, lng, lnb, *, tm):
    batch, seq, _ = x.shape
    n_tiles = seq // tm
    n_groups = len(POOL_WINDOWS)
    bsum, bhalo = _pool_band_matrices()
    rows = np.arange(HEADS_PER_LANE_TILE * 2 * BLOCK)[:, None] // (2 * BLOCK)
    ones_pat = (rows == np.arange(LANES)[None, :] // HEAD_DIM).astype(np.float32)

    def const_spec(shape):
        zeros = (0,) * len(shape)
        return pl.BlockSpec(shape, lambda b, i: zeros, pipeline_mode=pl.Buffered(1))

    hbm_spec = pl.BlockSpec(memory_space=pl.ANY)

    return pl.pallas_call(
        functools.partial(_layer_kernel, tm=tm, layer=layer),
        grid=(batch, n_tiles + 1),
        in_specs=[
            pl.BlockSpec((1, tm, D_MODEL), lambda b, i: (b, jnp.minimum(i, n_tiles - 1), 0)),
            pl.BlockSpec((1, 3, batch, D_MODEL), lambda b, i: (layer, 0, 0, 0),
                         pipeline_mode=pl.Buffered(1)),
            pl.BlockSpec(memory_space=pltpu.SMEM),
            pl.BlockSpec(memory_space=pltpu.SMEM),
            const_spec((BLOCK, 2 * BLOCK)),
            const_spec((n_groups, POOL_ROWS, POOL_ROWS)),
            const_spec((n_groups, POOL_HALO, POOL_HALO)),
            const_spec(ones_pat.shape),
            hbm_spec, hbm_spec, hbm_spec, hbm_spec, hbm_spec,
            const_spec((DEPTH, D_MODEL)),
            const_spec((DEPTH, D_MODEL)),
            const_spec((DEPTH, D_MODEL)),
        ],
        out_specs=pl.BlockSpec((1, tm, D_MODEL), lambda b, i: (b, jnp.maximum(i - 1, 0), 0)),
        out_shape=jax.ShapeDtypeStruct(x.shape, F32),
        scratch_shapes=[
            pltpu.VMEM((N_Q_HEADS, BLOCK, 2 * BLOCK), F32),
            pltpu.VMEM((D_MODEL, IN_WIDTH), BF16),
            pltpu.VMEM((D_MODEL, D_MODEL), BF16),
            pltpu.VMEM((D_MODEL, D_MODEL), BF16),
            pltpu.VMEM((D_MODEL, D_MODEL), BF16),
            pltpu.VMEM((n_groups * POOL_GROUP_WIDTH, POOL_GROUP_WIDTH), BF16),
            pltpu.VMEM((LOAD_SLOTS, STAGE_IN_ROWS, IN_WIDTH), F32),
            pltpu.VMEM((LOAD_SLOTS, STAGE_SQ_ROWS, D_MODEL), F32),
            pltpu.VMEM((n_groups * POOL_GROUP_WIDTH, POOL_GROUP_WIDTH), F32),
            pltpu.SemaphoreType.DMA((2 * LOAD_SLOTS + 1,)),
            pltpu.VMEM((tm, D_MODEL), BF16),
            pltpu.VMEM((tm, D_MODEL), BF16),
            pltpu.VMEM((4, tm + BLOCK, LANES), BF16),
            pltpu.VMEM((4, tm + BLOCK, LANES), BF16),
            pltpu.VMEM((LANE_TILES_PER_GROUP * BLOCK, HEADS_PER_LANE_TILE * 2 * BLOCK), BF16),
            pltpu.VMEM((tm, D_MODEL), F32),
            pltpu.VMEM((tm, D_MODEL), BF16),
            pltpu.VMEM((POOL_HALO, D_MODEL), BF16),
            pltpu.VMEM((tm, IN_WIDTH - OFF_PIN), F32),
            pltpu.VMEM((tm, D_MODEL), BF16),
            pltpu.VMEM((tm, D_MODEL), F32),
        ],
        compiler_params=pltpu.CompilerParams(
            dimension_semantics=("arbitrary", "arbitrary"),
            vmem_limit_bytes=VMEM_LIMIT_BYTES),
        name="fused_layer",
    )(x, mod, sinks, rel_bias, jnp.asarray(_bucket_map()), jnp.asarray(bsum, BF16),
      jnp.asarray(bhalo, BF16),
      jnp.asarray(ones_pat, BF16), w_in, w_ap, w_pp, w_out, w_mix, pscale, lng, lnb)


@functools.partial(jax.jit, static_argnames=("tm",))
def _forward(x, c, rel_bias, w_ada, b_ada, w_in, sinks, w_pool_mix, pool_scale,
             w_attn_proj, w_pool_proj, w_out, ln_gain, ln_bias, tm=512):
    batch = x.shape[0]
    mod = _adaln(c, w_ada, b_ada)
    w_mix = w_pool_mix.reshape(DEPTH, len(POOL_WINDOWS) * POOL_GROUP_WIDTH, POOL_GROUP_WIDTH)
    for l in range(DEPTH):
        x = _layer(l, x, mod, sinks, rel_bias, w_in, w_attn_proj, w_pool_proj, w_out, w_mix,
                   pool_scale, ln_gain, ln_bias, tm=tm)
    return x


def kernel(x, c, rel_bias, w_ada, b_ada, w_in, sinks, w_pool_mix, pool_scale, w_attn_proj,
           w_pool_proj, w_out, ln_gain, ln_bias):
    return _forward(x, c, rel_bias, w_ada, b_ada, w_in, sinks, w_pool_mix, pool_scale,
                    w_attn_proj, w_pool_proj, w_out, ln_gain, ln_bias)
```

```python
import functools
import math

import numpy as np
import jax
import jax.numpy as jnp
from jax import lax
from jax.experimental import pallas as pl
from jax.experimental.pallas import tpu as pltpu

D_MODEL = 1024
DEPTH = 2
N_Q_HEADS = 16
N_KV_HEADS = 2
HEAD_DIM = 64
Q_GROUP = N_Q_HEADS // N_KV_HEADS
WINDOW = 128
BLOCK = 128
NEG_INF = -1e30
NUM_BUCKETS = 32
MAX_DISTANCE = 128
POOL_WINDOWS = (2, 4, 8, 16)
POOL_GROUP_WIDTH = D_MODEL // len(POOL_WINDOWS)
POOL_HALO = 16
DEEPNORM_ALPHA = (2 * DEPTH) ** 0.25
LN_EPS = 1e-5
LOG2E = math.log2(math.e)
POOL_ROWS = 256
STAGE_IN_ROWS = 32
STAGE_SQ_ROWS = 128
LOAD_SLOTS = 4

OFF_Q = 0
OFF_KV = 1024
OFF_AGATE = 1280
OFF_PIN = 2304
OFF_PGATE = 3328
OFF_MERGE = 4352
IN_WIDTH = 6400

LANES = 128
HEADS_PER_LANE_TILE = LANES // HEAD_DIM
LANE_TILES_PER_GROUP = Q_GROUP // HEADS_PER_LANE_TILE
VMEM_LIMIT_BYTES = 60 * 1024 * 1024

F32 = jnp.float32
BF16 = jnp.bfloat16


def _bucket_map():
    q = np.arange(BLOCK)[:, None]
    k = np.arange(2 * BLOCK)[None, :]
    dist = np.maximum(q + BLOCK - k, 0)
    max_exact = NUM_BUCKETS // 2
    d = np.maximum(dist, 1).astype(np.float64)
    large = max_exact + (np.log(d / max_exact) / math.log(MAX_DISTANCE / max_exact)
                         * (NUM_BUCKETS - max_exact)).astype(np.int32)
    large = np.minimum(large, NUM_BUCKETS - 1)
    return np.where(dist < max_exact, dist, large).astype(np.int32)


def _sigmoid(x):
    return 1.0 / (1.0 + jnp.exp2(x * (-LOG2E)))


def _silu(x):
    return x * _sigmoid(x)


def _adaln_kernel(c_ref, w_ref, b_ref, o_ref):
    l = pl.program_id(0)
    sc = _silu(c_ref[...])
    o_ref[0, 0] = jnp.dot(sc, w_ref[0], preferred_element_type=F32) + b_ref[pl.ds(l, 1), :]


def _adaln(c, w_ada, b_ada):
    batch = c.shape[0]
    return pl.pallas_call(
        _adaln_kernel,
        grid=(DEPTH, 3),
        in_specs=[
            pl.BlockSpec((batch, D_MODEL), lambda l, n: (0, 0)),
            pl.BlockSpec((1, D_MODEL, D_MODEL), lambda l, n: (l, 0, n)),
            pl.BlockSpec((DEPTH, D_MODEL), lambda l, n: (0, n)),
        ],
        out_specs=pl.BlockSpec((1, 1, batch, D_MODEL), lambda l, n: (l, n, 0, 0)),
        out_shape=jax.ShapeDtypeStruct((DEPTH, 3, batch, D_MODEL), F32),
        name="adaln_mod",
    )(c, w_ada, b_ada)


BIAS_HEADS_PER_STEP = 8
BIAS_ROW_CHUNK = 32


def _bias_kernel(rel_ref, bucket_ref, o_ref):
    h0 = pl.program_id(0) * BIAS_HEADS_PER_STEP
    for r0 in range(0, BLOCK, BIAS_ROW_CHUNK):
        bucket = bucket_ref[r0:r0 + BIAS_ROW_CHUNK, :]
        acc = [jnp.zeros(bucket.shape, F32) for _ in range(BIAS_HEADS_PER_STEP)]
        for b in range(NUM_BUCKETS):
            hit = bucket == b
            for hh in range(BIAS_HEADS_PER_STEP):
                acc[hh] = jnp.where(hit, rel_ref[b, h0 + hh] * LOG2E, acc[hh])
        for hh in range(BIAS_HEADS_PER_STEP):
            o_ref[hh, r0:r0 + BIAS_ROW_CHUNK, :] = acc[hh]


def _bias_table(rel_bias):
    bucket = jnp.asarray(_bucket_map())
    return pl.pallas_call(
        _bias_kernel,
        grid=(N_Q_HEADS // BIAS_HEADS_PER_STEP,),
        in_specs=[
            pl.BlockSpec(memory_space=pltpu.SMEM),
            pl.BlockSpec((BLOCK, 2 * BLOCK), lambda h: (0, 0)),
        ],
        out_specs=pl.BlockSpec((BIAS_HEADS_PER_STEP, BLOCK, 2 * BLOCK), lambda h: (h, 0, 0)),
        out_shape=jax.ShapeDtypeStruct((N_Q_HEADS, BLOCK, 2 * BLOCK), F32),
        name="t5_bias_table",
    )(rel_bias, bucket)


def _layer_kernel(x_ref, mod_ref, sinks_ref, bias_ref, bsum_ref, bhalo_ref, ones_ref,
                  w_in_hbm, w_ap_hbm, w_pp_hbm, w_out_hbm, w_mix_hbm,
                  pscale_ref, lng_ref, lnb_ref, o_ref,
                  w_in_ref, w_ap_ref, w_pp_ref, w_out_ref, w_mix_ref,
                  stage_in_ref, stage_sq_ref, stage_mix_ref, load_sem,
                  u_ref, q_ref, kp_ref, vp_ref, p_ref, sg_ref, ain_ref, halo_ref, h_ref,
                  yin_ref, xprev_ref, *, tm, layer):
    i = pl.program_id(1)
    n_tiles = pl.num_programs(1) - 1
    nblk = tm // BLOCK
    n_chunks = tm // POOL_ROWS
    n_iter = nblk * N_KV_HEADS
    cw = (IN_WIDTH - OFF_PIN) // n_iter

    @pl.when(jnp.logical_and(pl.program_id(0) == 0, i == 0))
    def _():
        copies = []

        def add(hbm_rows, stage, sem, dst):
            copies.append((pltpu.make_async_copy(hbm_rows, stage, sem), stage, dst))

        for k in range(D_MODEL // STAGE_IN_ROWS):
            rows = slice(k * STAGE_IN_ROWS, (k + 1) * STAGE_IN_ROWS)
            slot = k % LOAD_SLOTS
            add(w_in_hbm.at[layer, rows, :], stage_in_ref.at[slot], load_sem.at[slot],
                w_in_ref.at[rows, :])
        n_sq = 0
        for hbm, dst in ((w_ap_hbm, w_ap_ref), (w_pp_hbm, w_pp_ref), (w_out_hbm, w_out_ref)):
            for k in range(D_MODEL // STAGE_SQ_ROWS):
                rows = slice(k * STAGE_SQ_ROWS, (k + 1) * STAGE_SQ_ROWS)
                slot = n_sq % LOAD_SLOTS
                n_sq += 1
                add(hbm.at[layer, rows, :], stage_sq_ref.at[slot], load_sem.at[LOAD_SLOTS + slot],
                    dst.at[rows, :])
        add(w_mix_hbm.at[layer], stage_mix_ref, load_sem.at[2 * LOAD_SLOTS], w_mix_ref)

        for k in range(min(LOAD_SLOTS, len(copies))):
            copies[k][0].start()
        for k, (copy, stage, dst) in enumerate(copies):
            copy.wait()
            dst[...] = stage[...].astype(BF16)
            if k + LOAD_SLOTS < len(copies):
                copies[k + LOAD_SLOTS][0].start()

    def mod_row(kind):
        return mod_ref[0, kind, pl.ds(pl.program_id(0), 1), :]

    @pl.when(i == 0)
    def _():
        kp_ref[:, 0:BLOCK, :] = jnp.zeros((4, BLOCK, LANES), BF16)
        vp_ref[:, 0:BLOCK, :] = jnp.zeros((4, BLOCK, LANES), BF16)
        halo_ref[...] = jnp.zeros((POOL_HALO, D_MODEL), BF16)
        yin_ref[...] = jnp.zeros((tm, D_MODEL), BF16)
        xprev_ref[...] = jnp.zeros((tm, D_MODEL), F32)

    def proj(lo, hi):
        return jnp.dot(u_ref[...], w_in_ref[:, lo:hi], preferred_element_type=F32)

    def finish_previous_tile():
        gate1 = 1.0 + mod_row(2)
        gain = lng_ref[layer:layer + 1, :]
        beta = lnb_ref[layer:layer + 1, :]
        for c in range(n_chunks):
            rs = slice(c * POOL_ROWS, (c + 1) * POOL_ROWS)
            y = jnp.dot(yin_ref[rs, :], w_out_ref[...], preferred_element_type=F32) * gate1
            r = DEEPNORM_ALPHA * xprev_ref[rs, :] + y
            mu = jnp.mean(r, axis=-1, keepdims=True)
            rc = r - mu
            var = jnp.mean(rc * rc, axis=-1, keepdims=True)
            o_ref[0, rs, :] = rc * lax.rsqrt(var + LN_EPS) * gain + beta

    def tile_head():
        x = x_ref[0]
        u_ref[...] = (x * (1.0 + mod_row(1)) + mod_row(0)).astype(BF16)
        xprev_ref[...] = x

        kv = proj(OFF_KV, OFF_AGATE)
        lane = lax.broadcasted_iota(jnp.int32, (tm, LANES), 1)
        first_half = lane < HEAD_DIM
        for src, dst in ((kv[:, 0:LANES], kp_ref), (kv[:, LANES:2 * LANES], vp_ref)):
            g0_lo = jnp.where(first_half, src, 0.0)
            g1_hi = jnp.where(first_half, 0.0, src)
            dst[0, BLOCK:, :] = g0_lo.astype(BF16)
            dst[1, BLOCK:, :] = pltpu.roll(g0_lo, HEAD_DIM, 1).astype(BF16)
            dst[2, BLOCK:, :] = pltpu.roll(g1_hi, HEAD_DIM, 1).astype(BF16)
            dst[3, BLOCK:, :] = g1_hi.astype(BF16)
        q_ref[...] = (proj(OFF_Q, OFF_KV) * (HEAD_DIM ** -0.5 * LOG2E)).astype(BF16)

    def tile_rest():
        sg_ref[...] = _silu(proj(OFF_AGATE, OFF_PIN))

        qi = lax.broadcasted_iota(jnp.int32, (BLOCK, 2 * BLOCK), 0)
        ki = lax.broadcasted_iota(jnp.int32, (BLOCK, 2 * BLOCK), 1)
        dist = qi + BLOCK - ki
        band = (dist >= 0) & (dist < WINDOW)
        band_first = band & (ki >= jnp.where(i == 0, BLOCK, 0))
        lane_b = lax.broadcasted_iota(jnp.int32, (BLOCK, LANES), 1)
        first_half_b = lane_b < HEAD_DIM

        def deferred_projection(it):
            hv = proj(OFF_PIN + it * cw, OFF_PIN + (it + 1) * cw)
            segment, seg_off = divmod(it * cw, D_MODEL)
            if segment == 1:
                hv = _silu(hv) * pscale_ref[layer:layer + 1, seg_off:seg_off + cw]
            elif segment >= 2:
                hv = _sigmoid(hv)
            h_ref[:, it * cw:(it + 1) * cw] = hv

        def qk_logits(qb, g):
            r0 = qb * BLOCK
            c0 = g * LANE_TILES_PER_GROUP * LANES
            qs = jnp.concatenate(
                [q_ref[r0:r0 + BLOCK, c0 + j * LANES:c0 + (j + 1) * LANES]
                 for j in range(LANE_TILES_PER_GROUP)], axis=0)
            kcat = jnp.concatenate([kp_ref[2 * g, r0:r0 + 2 * BLOCK, :],
                                    kp_ref[2 * g + 1, r0:r0 + 2 * BLOCK, :]], axis=0)
            return lax.dot_general(qs, kcat, (((1,), (1,)), ((), ())), preferred_element_type=F32)

        def softmax_numerators(qb, g, s):
            mask = band_first if qb == 0 else band
            sink_p = {}
            for j in range(LANE_TILES_PER_GROUP):
                for par in range(HEADS_PER_LANE_TILE):
                    h = g * Q_GROUP + j * HEADS_PER_LANE_TILE + par
                    sh = s[j * BLOCK:(j + 1) * BLOCK, par * 2 * BLOCK:(par + 1) * 2 * BLOCK]
                    logits = jnp.where(mask, sh + bias_ref[h], NEG_INF)
                    sink = sinks_ref[layer, h] * LOG2E
                    m = jnp.maximum(jnp.max(logits, axis=-1, keepdims=True), sink)
                    p_ref[j * BLOCK:(j + 1) * BLOCK,
                          par * 2 * BLOCK:(par + 1) * 2 * BLOCK] = jnp.exp2(logits - m).astype(BF16)
                    sink_p[(j, par)] = jnp.exp2(sink - m)
            return sink_p

        def attend(qb, g, sink_p):
            r0 = qb * BLOCK
            c0 = g * LANE_TILES_PER_GROUP * LANES
            vcat = jnp.concatenate(
                [jnp.concatenate([vp_ref[2 * g, r0:r0 + 2 * BLOCK, :],
                                  vp_ref[2 * g + 1, r0:r0 + 2 * BLOCK, :]], axis=0),
                 ones_ref[...]], axis=1)
            o = jnp.dot(p_ref[...], vcat, preferred_element_type=F32)
            for j in range(LANE_TILES_PER_GROUP):
                cj = c0 + j * LANES
                oj = o[j * BLOCK:(j + 1) * BLOCK, :]
                denom = oj[:, LANES:] + jnp.where(first_half_b, sink_p[(j, 0)], sink_p[(j, 1)])
                ain_ref[r0:r0 + BLOCK, cj:cj + LANES] = (
                    oj[:, 0:LANES] / denom * sg_ref[r0:r0 + BLOCK, cj:cj + LANES]).astype(BF16)

        pool = {"halo": halo_ref[...]}

        def pool_window_sums(c):
            rs = slice(c * POOL_ROWS, (c + 1) * POOL_ROWS)
            pin = h_ref[rs, 0:D_MODEL]
            pin_b = pin.astype(BF16)
            tpos = (i * tm + c * POOL_ROWS + 1
                    + lax.broadcasted_iota(jnp.int32, (POOL_ROWS, 1), 0)).astype(F32)
            pooled = []
            for gi, w in enumerate(POOL_WINDOWS):
                cs = slice(gi * POOL_GROUP_WIDTH, (gi + 1) * POOL_GROUP_WIDTH)
                wsum = jnp.dot(bsum_ref[gi], pin_b[:, cs], preferred_element_type=F32)
                head = wsum[0:POOL_HALO] + jnp.dot(bhalo_ref[gi], pool["halo"][:, cs],
                                                   preferred_element_type=F32)
                wsum = jnp.concatenate([head, wsum[POOL_HALO:]], axis=0)
                pooled.append((wsum / jnp.minimum(tpos, float(w)) - pin[:, cs]).astype(BF16))
            pool["halo"] = pin_b[POOL_ROWS - POOL_HALO:, :]
            pool[("pooled", c)] = pooled

        def pool_mix(c):
            rs = slice(c * POOL_ROWS, (c + 1) * POOL_ROWS)
            pooled = pool.pop(("pooled", c))
            mixed = [jnp.dot(pooled[gi],
                             w_mix_ref[gi * POOL_GROUP_WIDTH:(gi + 1) * POOL_GROUP_WIDTH, :],
                             preferred_element_type=F32)
                     for gi in range(len(POOL_WINDOWS))]
            pool[("pp", c)] = (jnp.concatenate(mixed, axis=1)
                               * h_ref[rs, D_MODEL:2 * D_MODEL]).astype(BF16)

        def pool_project(c):
            pool[("p_proj", c)] = jnp.dot(pool.pop(("pp", c)), w_pp_ref[...],
                                          preferred_element_type=F32)

        def attn_project(c):
            rs = slice(c * POOL_ROWS, (c + 1) * POOL_ROWS)
            pool[("a_proj", c)] = jnp.dot(ain_ref[rs, :], w_ap_ref[...], preferred_element_type=F32)

        def merge(c):
            rs = slice(c * POOL_ROWS, (c + 1) * POOL_ROWS)
            yin_ref[rs, :] = (h_ref[rs, 2 * D_MODEL:3 * D_MODEL] * pool.pop(("a_proj", c))
                              + h_ref[rs, 3 * D_MODEL:4 * D_MODEL] * pool.pop(("p_proj", c))
                              ).astype(BF16)

        extra = {it: [] for it in range(n_iter)}
        pin_done = n_iter // 4
        gate_done = n_iter // 2
        for c in range(n_chunks):
            extra[pin_done].append(functools.partial(pool_window_sums, c))
            extra[pin_done + 1].append(functools.partial(pool_mix, c))
            extra[min(gate_done + c, n_iter - 1)].append(functools.partial(pool_project, c))
        for c in range(n_chunks - 1):
            extra[min(4 * c + 7, n_iter - 1)].append(functools.partial(attn_project, c))

        for qb in range(nblk):
            for g in range(N_KV_HEADS):
                it = qb * N_KV_HEADS + g
                last = it == n_iter - 1
                s = qk_logits(qb, g)
                if not last:
                    deferred_projection(it)
                for fn in extra[it]:
                    fn()
                sink_p = softmax_numerators(qb, g, s)
                attend(qb, g, sink_p)
                if last:
                    deferred_projection(it)
        attn_project(n_chunks - 1)

        kp_ref[:, 0:BLOCK, :] = kp_ref[:, tm:tm + BLOCK, :]
        vp_ref[:, 0:BLOCK, :] = vp_ref[:, tm:tm + BLOCK, :]
        halo_ref[...] = pool["halo"]
        for c in range(n_chunks):
            merge(c)

    @pl.when(i < n_tiles)
    def _():
        finish_previous_tile()
        tile_head()

    @pl.when(jnp.logical_and(i >= 0, i < n_tiles))
    def _():
        tile_rest()

    @pl.when(i == n_tiles)
    def _():
        finish_previous_tile()


def _pool_band_matrices():
    tm = POOL_ROWS
    t = np.arange(tm)[:, None]
    own = np.stack([((t - np.arange(tm)[None, :] >= 0) & (t - np.arange(tm)[None, :] < w))
                    for w in POOL_WINDOWS])
    th = np.arange(POOL_HALO)[:, None]
    back = th - (np.arange(POOL_HALO)[None, :] - POOL_HALO)
    halo = np.stack([back < w for w in POOL_WINDOWS])
    return own.astype(np.float32), halo.astype(np.float32)


def _layer(layer, x, mod, sinks, bias, w_in, w_ap, w_pp, w_out, w_mix, pscale, lng, lnb, *, tm):
    batch, seq, _ = x.shape
    n_tiles = seq // tm
    n_groups = len(POOL_WINDOWS)
    bsum, bhalo = _pool_band_matrices()
    rows = np.arange(HEADS_PER_LANE_TILE * 2 * BLOCK)[:, None] // (2 * BLOCK)
    ones_pat = (rows == np.arange(LANES)[None, :] // HEAD_DIM).astype(np.float32)

    def const_spec(shape):
        zeros = (0,) * len(shape)
        return pl.BlockSpec(shape, lambda b, i: zeros, pipeline_mode=pl.Buffered(1))

    hbm_spec = pl.BlockSpec(memory_space=pl.ANY)

    return pl.pallas_call(
        functools.partial(_layer_kernel, tm=tm, layer=layer),
        grid=(batch, n_tiles + 1),
        in_specs=[
            pl.BlockSpec((1, tm, D_MODEL), lambda b, i: (b, jnp.minimum(i, n_tiles - 1), 0)),
            pl.BlockSpec((1, 3, batch, D_MODEL), lambda b, i: (layer, 0, 0, 0),
                         pipeline_mode=pl.Buffered(1)),
            pl.BlockSpec(memory_space=pltpu.SMEM),
            const_spec((N_Q_HEADS, BLOCK, 2 * BLOCK)),
            const_spec((n_groups, POOL_ROWS, POOL_ROWS)),
            const_spec((n_groups, POOL_HALO, POOL_HALO)),
            const_spec(ones_pat.shape),
            hbm_spec, hbm_spec, hbm_spec, hbm_spec, hbm_spec,
            const_spec((DEPTH, D_MODEL)),
            const_spec((DEPTH, D_MODEL)),
            const_spec((DEPTH, D_MODEL)),
        ],
        out_specs=pl.BlockSpec((1, tm, D_MODEL), lambda b, i: (b, jnp.maximum(i - 1, 0), 0)),
        out_shape=jax.ShapeDtypeStruct(x.shape, F32),
        scratch_shapes=[
            pltpu.VMEM((D_MODEL, IN_WIDTH), BF16),
            pltpu.VMEM((D_MODEL, D_MODEL), BF16),
            pltpu.VMEM((D_MODEL, D_MODEL), BF16),
            pltpu.VMEM((D_MODEL, D_MODEL), BF16),
            pltpu.VMEM((n_groups * POOL_GROUP_WIDTH, POOL_GROUP_WIDTH), BF16),
            pltpu.VMEM((LOAD_SLOTS, STAGE_IN_ROWS, IN_WIDTH), F32),
            pltpu.VMEM((LOAD_SLOTS, STAGE_SQ_ROWS, D_MODEL), F32),
            pltpu.VMEM((n_groups * POOL_GROUP_WIDTH, POOL_GROUP_WIDTH), F32),
            pltpu.SemaphoreType.DMA((2 * LOAD_SLOTS + 1,)),
            pltpu.VMEM((tm, D_MODEL), BF16),
            pltpu.VMEM((tm, D_MODEL), BF16),
            pltpu.VMEM((4, tm + BLOCK, LANES), BF16),
            pltpu.VMEM((4, tm + BLOCK, LANES), BF16),
            pltpu.VMEM((LANE_TILES_PER_GROUP * BLOCK, HEADS_PER_LANE_TILE * 2 * BLOCK), BF16),
            pltpu.VMEM((tm, D_MODEL), F32),
            pltpu.VMEM((tm, D_MODEL), BF16),
            pltpu.VMEM((POOL_HALO, D_MODEL), BF16),
            pltpu.VMEM((tm, IN_WIDTH - OFF_PIN), F32),
            pltpu.VMEM((tm, D_MODEL), BF16),
            pltpu.VMEM((tm, D_MODEL), F32),
        ],
        compiler_params=pltpu.CompilerParams(
            dimension_semantics=("arbitrary", "arbitrary"),
            vmem_limit_bytes=VMEM_LIMIT_BYTES),
        name="fused_layer",
    )(x, mod, sinks, bias, jnp.asarray(bsum, BF16), jnp.asarray(bhalo, BF16),
      jnp.asarray(ones_pat, BF16), w_in, w_ap, w_pp, w_out, w_mix, pscale, lng, lnb)


@functools.partial(jax.jit, static_argnames=("tm",))
def _forward(x, c, rel_bias, w_ada, b_ada, w_in, sinks, w_pool_mix, pool_scale,
             w_attn_proj, w_pool_proj, w_out, ln_gain, ln_bias, tm=256):
    batch = x.shape[0]
    mod = _adaln(c, w_ada, b_ada)
    bias = _bias_table(rel_bias)
    w_mix = w_pool_mix.reshape(DEPTH, len(POOL_WINDOWS) * POOL_GROUP_WIDTH, POOL_GROUP_WIDTH)
    for l in range(DEPTH):
        x = _layer(l, x, mod, sinks, bias, w_in, w_attn_proj, w_pool_proj, w_out, w_mix,
                   pool_scale, ln_gain, ln_bias, tm=tm)
    return x


def kernel(x, c, rel_bias, w_ada, b_ada, w_in, sinks, w_pool_mix, pool_scale, w_attn_proj,
           w_pool_proj, w_out, ln_gain, ln_bias):
    return _forward(x, c, rel_bias, w_ada, b_ada, w_in, sinks, w_pool_mix, pool_scale,
                    w_attn_proj, w_pool_proj, w_out, ln_gain, ln_bias)
```
